```python
import math
import jax
import jax.numpy as jnp
from jax import lax
import numpy as np

D_MODEL = 1024
BATCH = 4
SEQ = 4096
DEPTH = 2

GRID_W = 64
CTX_LEN = 256
EPS = 1e-6
F32 = jnp.float32

SSD_HEADS = 12
SSD_HEAD_DIM = 64
SSD_WIDTH = SSD_HEADS * SSD_HEAD_DIM
SSD_GROUPS = 2
SSD_REP = SSD_HEADS // SSD_GROUPS
SSD_STATE = 128
SSD_BC = SSD_GROUPS * SSD_STATE
SSD_XBC = SSD_WIDTH + 2 * SSD_BC
SSD_CONV_LEN = 5
SSD_CHUNK = 128
ATTN_HEADS = 12
ATTN_KV_HEADS = 4
ATTN_REP = ATTN_HEADS // ATTN_KV_HEADS
HEAD_DIM = 64
ATTN_WIDTH = ATTN_HEADS * HEAD_DIM
KV_WIDTH = ATTN_KV_HEADS * HEAD_DIM
Q_BLOCK = 128
ROPE_BASE = 10000.0
ROPE_NF = HEAD_DIM // 4
CM_WIDTH = 512
CM_CONV_LEN = 31

MIX_WIDTH = SSD_WIDTH + ATTN_WIDTH + CM_WIDTH
IN_SIZES = (SSD_WIDTH, SSD_XBC, 2 * SSD_HEADS, ATTN_WIDTH, KV_WIDTH, KV_WIDTH, ATTN_WIDTH, CM_WIDTH, CM_WIDTH, CM_WIDTH)
IN_WIDTH = sum(IN_SIZES)
IN_OFFSETS = tuple(sum(IN_SIZES[:i + 1]) for i in range(len(IN_SIZES) - 1))

kernel_name = 'hybrid_ssd_gqa_conformer_prefix_dit'


def rmsnorm(x, g):
    xf = x.astype(F32)
    y = xf * lax.rsqrt(jnp.mean(xf * xf, axis=-1, keepdims=True) + EPS)
    return (y * g.astype(F32)).astype(x.dtype)


def layernorm(x, g, b):
    xf = x.astype(F32)
    xc = xf - jnp.mean(xf, axis=-1, keepdims=True)
    y = xc * lax.rsqrt(jnp.mean(xc * xc, axis=-1, keepdims=True) + EPS)
    return (y * g.astype(F32) + b.astype(F32)).astype(x.dtype)


def modulate(x, g, shift, scale):
    return rmsnorm(x, g) * (1 + scale) + shift


def dwconv(x, w, b):
    pad = w.shape[0] // 2
    y = lax.conv_general_dilated(x, w[:, None, :].astype(x.dtype), window_strides=(1,), padding=[(pad, pad)], dimension_numbers=('NWC', 'WIO', 'NWC'), feature_group_count=x.shape[-1])
    return y + b.astype(x.dtype)


def split_cols(p):
    return jnp.split(p, IN_OFFSETS, axis=-1)


def axial_rope_angles(seq):
    n_rows = seq // GRID_W
    rows = jnp.repeat(jnp.arange(n_rows, dtype=F32), GRID_W)
    cols = jnp.tile(jnp.arange(GRID_W, dtype=F32), n_rows)
    inv = ROPE_BASE ** (-jnp.arange(ROPE_NF, dtype=F32) / ROPE_NF)
    ang = jnp.stack([rows[:, None] * inv, cols[:, None] * inv], axis=1)
    return jnp.cos(ang), jnp.sin(ang)


def apply_axial_rope(x, cos, sin):
    b, s, h, d = x.shape
    xf = x.astype(F32).reshape(b, s, h, 2, 2, ROPE_NF)
    x1, x2 = xf[..., 0, :], xf[..., 1, :]
    cs, sn = cos[None, :, None], sin[None, :, None]
    out = jnp.stack([x1 * cs - x2 * sn, x2 * cs + x1 * sn], axis=-2)
    return out.reshape(b, s, h, d).astype(x.dtype)


def attend(q, keys, vals):
    b, L = q.shape[:2]
    nb = L // Q_BLOCK
    qb = q.reshape(b, nb, Q_BLOCK, ATTN_KV_HEADS, ATTN_REP, HEAD_DIM).transpose(1, 0, 2, 3, 4, 5)
    scale = HEAD_DIM ** -0.5

    def one_block(qi):
        s = jnp.einsum('bqkrd,bskd->bkrqs', qi, keys, preferred_element_type=F32) * scale
        p = jax.nn.softmax(s, axis=-1).astype(vals.dtype)
        return jnp.einsum('bkrqs,bskd->bqkrd', p, vals)

    out = lax.map(one_block, qb)
    return out.transpose(1, 0, 2, 3, 4, 5).reshape(b, L, ATTN_WIDTH)


def ssd_chunk_scan(x, dt, A, Bm, Cm, h0, return_y):
    b, L = x.shape[:2]
    nc, l = L // SSD_CHUNK, SSD_CHUNK
    G, R, P, N = SSD_GROUPS, SSD_REP, SSD_HEAD_DIM, SSD_STATE
    xdt = (x.astype(F32) * dt[..., None]).reshape(b, nc, l, G, R, P)
    Bc = Bm.astype(F32).reshape(b, nc, l, G, N)
    Cc = Cm.astype(F32).reshape(b, nc, l, G, N)
    cum = jnp.cumsum((dt * A).reshape(b, nc, l, G, R), axis=2)
    total = cum[:, :, -1]
    states = jnp.einsum('bclgn,bclgr,bclgrp->bcgrpn', Bc, jnp.exp(total[:, :, None] - cum), xdt)

    def step(h, inp):
        st, tot = inp
        return h * jnp.exp(tot)[..., None, None] + st, h

    h_final, h_in = lax.scan(step, h0, (jnp.moveaxis(states, 1, 0), jnp.moveaxis(total, 1, 0)))
    if not return_y:
        return None, h_final
    h_in = jnp.moveaxis(h_in, 0, 1)
    seg = cum[:, :, :, None] - cum[:, :, None, :]
    mask = jnp.tril(jnp.ones((l, l), dtype=bool))[None, None, :, :, None, None]
    lmat = jnp.exp(jnp.where(mask, seg, -jnp.inf))
    cb = jnp.einsum('bclgn,bcsgn->bclsg', Cc, Bc)
    y_diag = jnp.einsum('bclsg,bclsgr,bcsgrp->bclgrp', cb, lmat, xdt)
    y_off = jnp.einsum('bclgn,bcgrpn,bclgr->bclgrp', Cc, h_in, jnp.exp(cum))
    return (y_diag + y_off).reshape(b, L, G * R, P), h_final


def ssd_branch(z, xbc, dt_raw, h0_f, h0_b, conv_w, conv_b, dt_bias, a_log, d_skip, norm_g, return_y):
    b, L, _ = xbc.shape
    xbc = jax.nn.silu(dwconv(xbc, conv_w, conv_b))
    xs = xbc[..., :SSD_WIDTH].reshape(b, L, SSD_HEADS, SSD_HEAD_DIM)
    Bm = xbc[..., SSD_WIDTH:SSD_WIDTH + SSD_BC].reshape(b, L, SSD_GROUPS, SSD_STATE)
    Cm = xbc[..., SSD_WIDTH + SSD_BC:].reshape(b, L, SSD_GROUPS, SSD_STATE)
    dt = jax.nn.softplus(dt_raw.astype(F32) + dt_bias.astype(F32))
    A = -jnp.exp(a_log.astype(F32))
    H = SSD_HEADS
    flip = lambda t: jnp.flip(t, axis=1)
    y_f, h_f = ssd_chunk_scan(xs, dt[..., :H], A[:H], Bm, Cm, h0_f, return_y)
    y_b, h_b = ssd_chunk_scan(flip(xs), flip(dt[..., H:]), A[H:], flip(Bm), flip(Cm), h0_b, return_y)
    if not return_y:
        return None, h_f, h_b
    y = y_f + flip(y_b) + d_skip.astype(F32)[:, None] * xs.astype(F32)
    y = y.reshape(b, L, SSD_WIDTH) * jax.nn.silu(z.astype(F32))
    return rmsnorm(y, norm_g).astype(z.dtype), h_f, h_b


def conformer_branch(ua, ub, gate, conv_w, conv_b, ln_g, ln_b, pw_w, pw_b):
    u = ua * jax.nn.sigmoid(ub)
    u = jax.nn.silu(layernorm(dwconv(u, conv_w, conv_b), ln_g, ln_b))
    u = u @ pw_w + pw_b
    return u * jax.nn.silu(gate)


def setup_inputs(seed: int = 0) -> dict:
    key = jax.random.key(seed)
    ks = jax.random.split(key, 24)
    L, D = DEPTH, D_MODEL
    nrm = lambda k, shape, s: jax.random.normal(k, shape, F32) * s
    gain = lambda k, shape: 1.0 + 0.02 * jax.random.normal(k, shape, F32)
    dt0 = jnp.exp(jax.random.uniform(ks[10], (L, 2 * SSD_HEADS), F32, math.log(1e-3), math.log(1e-1)))
    return {
        'x': nrm(ks[0], (BATCH, SEQ, D), 1.0),
        'c': nrm(ks[1], (BATCH, D), 1.0),
        'ctx': nrm(ks[2], (BATCH, CTX_LEN, D), 1.0),
        'c_ctx': nrm(ks[3], (D,), 1.0),
        'norm_g': gain(ks[4], (L, D)),
        'ada_w': nrm(ks[5], (L, D, 3 * D), 0.5 * D ** -0.5),
        'ada_b': nrm(ks[6], (L, 3 * D), 0.02),
        'w_in': nrm(ks[7], (L, D, IN_WIDTH), D ** -0.5),
        'ssd_conv_w': nrm(ks[8], (L, SSD_CONV_LEN, SSD_XBC), SSD_CONV_LEN ** -0.5),
        'ssd_conv_b': nrm(ks[9], (L, SSD_XBC), 0.02),
        'ssd_dt_bias': dt0 + jnp.log(-jnp.expm1(-dt0)),
        'ssd_a_log': jnp.log(jax.random.uniform(ks[11], (L, 2 * SSD_HEADS), F32, 1.0, 16.0)),
        'ssd_d': gain(ks[12], (L, SSD_HEADS)),
        'ssd_norm_g': gain(ks[13], (L, SSD_WIDTH)),
        'q_norm_g': gain(ks[14], (L, HEAD_DIM)),
        'k_norm_g': gain(ks[15], (L, HEAD_DIM)),
        'cm_conv_w': nrm(ks[16], (L, CM_CONV_LEN, CM_WIDTH), CM_CONV_LEN ** -0.5),
        'cm_conv_b': nrm(ks[17], (L, CM_WIDTH), 0.02),
        'cm_ln_g': gain(ks[18], (L, CM_WIDTH)),
        'cm_ln_b': nrm(ks[19], (L, CM_WIDTH), 0.02),
        'cm_pw_w': nrm(ks[20], (L, CM_WIDTH, CM_WIDTH), CM_WIDTH ** -0.5),
        'cm_pw_b': nrm(ks[21], (L, CM_WIDTH), 0.02),
        'w_out': nrm(ks[22], (L, MIX_WIDTH, D), MIX_WIDTH ** -0.5),
        'final_norm_g': gain(ks[23], (D,)),
    }


def reference(x, c, ctx, c_ctx, norm_g, ada_w, ada_b, w_in, ssd_conv_w, ssd_conv_b, ssd_dt_bias, ssd_a_log, ssd_d, ssd_norm_g, q_norm_g, k_norm_g, cm_conv_w, cm_conv_b, cm_ln_g, cm_ln_b, cm_pw_w, cm_pw_b, w_out, final_norm_g):
    b, seq, _ = x.shape
    n_ctx = ctx.shape[1]
    cos, sin = axial_rope_angles(seq)
    silu_c, silu_cc = jax.nn.silu(c), jax.nn.silu(c_ctx)
    h0 = jnp.zeros((b, SSD_GROUPS, SSD_REP, SSD_HEAD_DIM, SSD_STATE), F32)
    for i in range(DEPTH):
        last = i == DEPTH - 1
        shift, scale, gate = jnp.split(silu_c @ ada_w[i] + ada_b[i], 3, axis=-1)
        shift_c, scale_c, gate_c = jnp.split(silu_cc @ ada_w[i] + ada_b[i], 3, axis=-1)
        h = modulate(x, norm_g[i], shift[:, None], scale[:, None])
        h_c = modulate(ctx, norm_g[i], shift_c, scale_c)
        z, xbc, dtr, q, k, v, ga, ua, ub, gc = split_cols(h @ w_in[i])
        z_c, xbc_c, dtr_c, q_c, k_c, v_c, ga_c, ua_c, ub_c, gc_c = split_cols(h_c @ w_in[i])
        ssd_p = (ssd_conv_w[i], ssd_conv_b[i], ssd_dt_bias[i], ssd_a_log[i], ssd_d[i], ssd_norm_g[i])
        cm_p = (cm_conv_w[i], cm_conv_b[i], cm_ln_g[i], cm_ln_b[i], cm_pw_w[i], cm_pw_b[i])
        ssd_ctx, hf, hb = ssd_branch(z_c, xbc_c, dtr_c, h0, h0, *ssd_p, return_y=not last)
        ssd_lat, _, _ = ssd_branch(z, xbc, dtr, hf, hb, *ssd_p, return_y=True)
        kc = rmsnorm(k_c.reshape(b, n_ctx, ATTN_KV_HEADS, HEAD_DIM), k_norm_g[i])
        vc = v_c.reshape(b, n_ctx, ATTN_KV_HEADS, HEAD_DIM)
        ql = apply_axial_rope(rmsnorm(q.reshape(b, seq, ATTN_HEADS, HEAD_DIM), q_norm_g[i]), cos, sin)
        kl = apply_axial_rope(rmsnorm(k.reshape(b, seq, ATTN_KV_HEADS, HEAD_DIM), k_norm_g[i]), cos, sin)
        vl = v.reshape(b, seq, ATTN_KV_HEADS, HEAD_DIM)
        attn_lat = attend(ql, jnp.concatenate([kl, kc], axis=1), jnp.concatenate([vl, vc], axis=1)) * jax.nn.silu(ga)
        conv_lat = conformer_branch(ua, ub, gc, *cm_p)
        y = jnp.concatenate([ssd_lat, attn_lat, conv_lat], axis=-1) @ w_out[i]
        x = x + gate[:, None] * y
        if not last:
            qc = rmsnorm(q_c.reshape(b, n_ctx, ATTN_HEADS, HEAD_DIM), q_norm_g[i])
            attn_ctx = attend(qc, kc, vc) * jax.nn.silu(ga_c)
            conv_ctx = conformer_branch(ua_c, ub_c, gc_c, *cm_p)
            y_c = jnp.concatenate([ssd_ctx, attn_ctx, conv_ctx], axis=-1) @ w_out[i]
            ctx = ctx + gate_c * y_c
    return rmsnorm(x, final_norm_g)
```

```python
import functools
import math

import numpy as np
import jax
import jax.numpy as jnp
from jax import lax
from jax.experimental import pallas as pl
from jax.experimental.pallas import tpu as pltpu

F32 = jnp.float32
BF16 = jnp.bfloat16

EPS = 1e-6
GRID_W = 64
ROPE_BASE = 10000.0

SSD_HEADS = 12
SSD_HEAD_DIM = 64
SSD_WIDTH = SSD_HEADS * SSD_HEAD_DIM
SSD_GROUPS = 2
SSD_STATE = 128
SSD_BC = SSD_GROUPS * SSD_STATE
SSD_XBC = SSD_WIDTH + 2 * SSD_BC
SSD_CONV_LEN = 5
SSD_CHUNK = 128
ATTN_HEADS = 12
ATTN_KV_HEADS = 4
ATTN_REP = ATTN_HEADS // ATTN_KV_HEADS
HEAD_DIM = 64
ATTN_WIDTH = ATTN_HEADS * HEAD_DIM
KV_WIDTH = ATTN_KV_HEADS * HEAD_DIM
ROPE_NF = HEAD_DIM // 4
CM_WIDTH = 512
CM_CONV_LEN = 31
MIX_WIDTH = SSD_WIDTH + ATTN_WIDTH + CM_WIDTH

LANES = 128
DT_PAD = LANES
TILE = 256
SSD_HALO = 8
CM_HALO = 16
CTX_ROW = 4
MOD_ROWS = 8
VMEM_LIMIT = 56 * 1024 * 1024

PROJ_WIDTHS = (SSD_WIDTH, SSD_XBC, ATTN_WIDTH, KV_WIDTH, KV_WIDTH, ATTN_WIDTH,
               CM_WIDTH, CM_WIDTH, CM_WIDTH, DT_PAD)
PROJ_TOTAL = sum(PROJ_WIDTHS)


def _dot(a, b):
    return jnp.dot(a, b, preferred_element_type=F32)


def _silu(x):
    return x * jax.nn.sigmoid(x)


def _split3(x):
    hi = x.astype(BF16)
    r1 = x - hi.astype(F32)
    mid = r1.astype(BF16)
    lo = (r1 - mid.astype(F32)).astype(BF16)
    return hi, mid, lo


def _params(sem):
    return pltpu.CompilerParams(dimension_semantics=sem, vmem_limit_bytes=VMEM_LIMIT)


def _ada_kernel(c_ref, w_ref, b_ref, o_ref):
    s = _silu(c_ref[...])
    s_hi = s.astype(BF16)
    s_lo = (s - s_hi.astype(F32)).astype(BF16)
    w = w_ref[0]
    w_hi = w.astype(BF16)
    w_lo = (w - w_hi.astype(F32)).astype(BF16)
    o_ref[0] = _dot(s_hi, w_hi) + _dot(s_hi, w_lo) + _dot(s_lo, w_hi) + b_ref[0]


def _ada(cvec, ada_w, ada_b):
    depth, d, n3 = ada_w.shape
    tn = 1024
    return pl.pallas_call(
        _ada_kernel,
        out_shape=jax.ShapeDtypeStruct((depth, MOD_ROWS, n3), F32),
        grid=(depth, n3 // tn),
        in_specs=[
            pl.BlockSpec((MOD_ROWS, d), lambda l, j: (0, 0)),
            pl.BlockSpec((1, d, tn), lambda l, j: (l, 0, j)),
            pl.BlockSpec((1, 1, tn), lambda l, j: (l, 0, j)),
        ],
        out_specs=pl.BlockSpec((1, MOD_ROWS, tn), lambda l, j: (l, 0, j)),
        compiler_params=_params(("arbitrary", "arbitrary")),
        name="ada_mod",
    )(cvec, ada_w, ada_b.reshape(depth, 1, n3))


def _inproj_kernel(x_ref, mod_ref, g_ref, w_ref, *out_refs):
    d = x_ref.shape[-1]
    b = pl.program_id(0)
    i = pl.program_id(1)
    row = jnp.where(i == 0, CTX_ROW, b)
    m = mod_ref[pl.ds(row, 1), :]
    shift = m[:, :d]
    scale = m[:, d:2 * d]
    x = x_ref[0]
    ms = jnp.mean(x * x, axis=-1, keepdims=True)
    h = (x * lax.rsqrt(ms + EPS) * g_ref[...]) * (1.0 + scale) + shift
    hb = h.astype(BF16)
    off = 0
    for o_ref, wd in zip(out_refs, PROJ_WIDTHS):
        o_ref[0] = _dot(hb, w_ref[:, off:off + wd])
        off += wd


def _inproj(xall, mod_l, g, w_all):
    nb, t, d = xall.shape
    nt = t // TILE
    outs = tuple(jax.ShapeDtypeStruct((nb, t, wd), F32) for wd in PROJ_WIDTHS)
    return pl.pallas_call(
        _inproj_kernel,
        out_shape=outs,
        grid=(nb, nt),
        in_specs=[
            pl.BlockSpec((1, TILE, d), lambda b, i: (b, i, 0)),
            pl.BlockSpec(mod_l.shape, lambda b, i: (0, 0)),
            pl.BlockSpec((1, d), lambda b, i: (0, 0)),
            pl.BlockSpec(w_all.shape, lambda b, i: (0, 0)),
        ],
        out_specs=tuple(pl.BlockSpec((1, TILE, wd), lambda b, i: (b, i, 0)) for wd in PROJ_WIDTHS),
        compiler_params=_params(("arbitrary", "arbitrary")),
        name="in_proj",
    )(xall, mod_l, g.reshape(1, d), w_all)


def _ssd_prep_kernel(xc_ref, xp_ref, xn_ref, dt_ref, cw_ref, cb_ref, dtb_ref, alog_ref, e3_ref,
                     xs_ref, xdtf_ref, xdtb_ref, bm_ref, cm_ref, bt_ref, cum_ref, cumt_ref,
                     ext_ref):
    i = pl.program_id(1)
    nt = pl.num_programs(1)
    prev_ok = i >= 2
    next_ok = jnp.logical_and(i != 0, i != nt - 1)
    ext_ref[0:SSD_HALO, :] = jnp.where(prev_ok, xp_ref[0], 0.0)
    ext_ref[SSD_HALO:SSD_HALO + TILE, :] = xc_ref[0]
    ext_ref[SSD_HALO + TILE:, :] = jnp.where(next_ok, xn_ref[0], 0.0)
    pad = SSD_CONV_LEN // 2
    acc = jnp.broadcast_to(cb_ref[...], (TILE, SSD_XBC))
    for k in range(SSD_CONV_LEN):
        acc = acc + ext_ref[pl.ds(SSD_HALO - pad + k, TILE), :] * cw_ref[k:k + 1, :]
    y = _silu(acc)
    xs = y[:, :SSD_WIDTH]
    bm = y[:, SSD_WIDTH:SSD_WIDTH + SSD_BC]
    xs_ref[0] = xs.astype(BF16)
    bm_ref[0] = bm.astype(BF16)
    cm_ref[0] = y[:, SSD_WIDTH + SSD_BC:].astype(BF16)
    bt_ref[0] = bm.T.astype(BF16)

    lane = lax.broadcasted_iota(jnp.int32, (TILE, DT_PAD), 1)
    nh2 = 2 * SSD_HEADS
    z = dt_ref[0] + dtb_ref[...]
    dt = jnp.maximum(z, 0.0) + jnp.log1p(jnp.exp(-jnp.abs(z)))
    dt = jnp.where(lane < nh2, dt, 0.0)
    a = dt * (-jnp.exp(alog_ref[...]))
    a = jnp.where(lane < nh2, a, 0.0)

    d_hi, d_mid, d_lo = _split3(dt)
    d3 = (d_hi.astype(F32) + pltpu.roll(d_mid.astype(F32), nh2, axis=1)
          + pltpu.roll(d_lo.astype(F32), 2 * nh2, axis=1)).astype(BF16)
    dexp = _dot(d3, e3_ref[...])
    xdtf_ref[0] = (xs * dexp[:, :SSD_WIDTH]).astype(BF16)
    xdtb_ref[0] = (xs * dexp[:, SSD_WIDTH:]).astype(BF16)

    r = lax.broadcasted_iota(jnp.int32, (SSD_CHUNK, 3 * SSD_CHUNK), 0)
    c = lax.broadcasted_iota(jnp.int32, (SSD_CHUNK, 3 * SSD_CHUNK), 1) % SSD_CHUNK
    tri_l = jnp.where(c <= r, 1.0, 0.0).astype(BF16)
    tri_u = jnp.where(c >= r, 1.0, 0.0).astype(BF16)
    lane_c = lax.broadcasted_iota(jnp.int32, (SSD_CHUNK, DT_PAD), 1)
    cums = []
    for ch in range(TILE // SSD_CHUNK):
        a_c = a[ch * SSD_CHUNK:(ch + 1) * SSD_CHUNK, :]
        a3 = jnp.concatenate(_split3(a_c), axis=0)
        cums.append(jnp.where(lane_c < SSD_HEADS, _dot(tri_l, a3), _dot(tri_u, a3)))
    cum = jnp.concatenate(cums, axis=0)
    cum_ref[0] = cum
    cumt_ref[0] = cum.T[:cumt_ref.shape[1], :]


def _ssd_prep(xbc, dt_raw, conv_w, conv_b, dt_bias, a_log, e3):
    nb, t, _ = xbc.shape
    nt = t // TILE
    hb = TILE // SSD_HALO
    nhb = t // SSD_HALO
    cw = jnp.zeros((8, SSD_XBC), F32).at[:SSD_CONV_LEN].set(conv_w)
    pad1 = lambda v: jnp.zeros((1, DT_PAD), F32).at[0, :v.shape[0]].set(v)
    ctr = 32
    tok = lambda w, dt_: jax.ShapeDtypeStruct((nb, t, w), dt_)
    tile = lambda w: pl.BlockSpec((1, TILE, w), lambda b, i: (b, i, 0))
    const = lambda shp: pl.BlockSpec(shp, lambda b, i: tuple(0 for _ in shp))
    return pl.pallas_call(
        _ssd_prep_kernel,
        out_shape=(tok(SSD_WIDTH, BF16), tok(SSD_WIDTH, BF16), tok(SSD_WIDTH, BF16),
                   tok(SSD_BC, BF16), tok(SSD_BC, BF16),
                   jax.ShapeDtypeStruct((nb, SSD_BC, t), BF16),
                   tok(DT_PAD, F32),
                   jax.ShapeDtypeStruct((nb, ctr, t), F32)),
        grid=(nb, nt),
        in_specs=[
            tile(SSD_XBC),
            pl.BlockSpec((1, SSD_HALO, SSD_XBC), lambda b, i: (b, jnp.maximum(i * hb - 1, 0), 0)),
            pl.BlockSpec((1, SSD_HALO, SSD_XBC), lambda b, i: (b, jnp.minimum((i + 1) * hb, nhb - 1), 0)),
            tile(DT_PAD),
            const((8, SSD_XBC)), const((1, SSD_XBC)), const((1, DT_PAD)), const((1, DT_PAD)),
            const(e3.shape),
        ],
        out_specs=(tile(SSD_WIDTH), tile(SSD_WIDTH), tile(SSD_WIDTH), tile(SSD_BC), tile(SSD_BC),
                   pl.BlockSpec((1, SSD_BC, TILE), lambda b, i: (b, 0, i)),
                   tile(DT_PAD),
                   pl.BlockSpec((1, ctr, TILE), lambda b, i: (b, 0, i))),
        scratch_shapes=[pltpu.VMEM((TILE + 2 * SSD_HALO, SSD_XBC), F32)],
        compiler_params=_params(("arbitrary", "arbitrary")),
        name="ssd_prep",
    )(xbc, xbc, xbc, dt_raw, cw, conv_b.reshape(1, SSD_XBC), pad1(dt_bias), pad1(a_log), e3)


def _ssd_direction(xdt, cm, bt, cum, cumt, h_ref, col0, fwd):
    n = SSD_CHUNK
    li = lax.broadcasted_iota(jnp.int32, (n, n), 0)
    si = lax.broadcasted_iota(jnp.int32, (n, n), 1)
    mask = (li >= si) if fwd else (si >= li)
    lo_half = si < SSD_HEAD_DIM
    tot_row = n - 1 if fwd else 0
    hprev = h_ref[...]
    hb = hprev.astype(BF16)
    gw = SSD_WIDTH // SSD_GROUPS
    pairs = SSD_HEADS // SSD_GROUPS // 2
    y_slabs = []
    h_new = []
    for g in range(SSD_GROUPS):
        cg = cm[:, g * SSD_STATE:(g + 1) * SSD_STATE]
        btg = bt[g * SSD_STATE:(g + 1) * SSD_STATE, :]
        cb = _dot(cg, btg)
        yoff = _dot(cg, hb[:, g * gw:(g + 1) * gw])
        xw_slabs = []
        etot_slabs = []
        for jj in range(pairs):
            j = g * pairs + jj
            h0 = col0 + 2 * j
            cb0 = jnp.broadcast_to(cum[:, h0:h0 + 1], (n, n))
            cb1 = jnp.broadcast_to(cum[:, h0 + 1:h0 + 2], (n, n))
            l0 = jnp.exp(jnp.where(mask, cb0 - cumt[h0:h0 + 1, :], -jnp.inf))
            l1 = jnp.exp(jnp.where(mask, cb1 - cumt[h0 + 1:h0 + 2, :], -jnp.inf))
            mm = jnp.concatenate([(cb * l0).astype(BF16), (cb * l1).astype(BF16)], axis=1)
            slab = xdt[:, j * LANES:(j + 1) * LANES]
            zero = jnp.zeros_like(slab)
            ww = jnp.concatenate([jnp.where(lo_half, slab, zero), jnp.where(lo_half, zero, slab)], axis=0)
            ydiag = _dot(mm, ww)
            csel = jnp.where(lo_half, cb0, cb1)
            tot = csel[tot_row:tot_row + 1, :]
            y_slabs.append(ydiag + yoff[:, jj * LANES:(jj + 1) * LANES] * jnp.exp(csel))
            xw_slabs.append((slab.astype(F32) * jnp.exp(tot - csel)).astype(BF16))
            etot_slabs.append(jnp.exp(tot))
        xw = jnp.concatenate(xw_slabs, axis=1)
        etot = jnp.concatenate(etot_slabs, axis=1)
        h_new.append(hprev[:, g * gw:(g + 1) * gw] * etot + _dot(btg, xw))
    h_ref[...] = jnp.concatenate(h_new, axis=1)
    return jnp.concatenate(y_slabs, axis=1)


def _ssd_scan_kernel(xf_ref, cf_ref, btf_ref, cumf_ref, cumtf_ref,
                     xb_ref, cb_ref, btb_ref, cumb_ref, cumtb_ref,
                     yf_ref, yb_ref, hf_ref, hb_ref):
    @pl.when(pl.program_id(1) == 0)
    def _():
        hf_ref[...] = jnp.zeros_like(hf_ref)
        hb_ref[...] = jnp.zeros_like(hb_ref)

    yf_ref[0] = _ssd_direction(xf_ref[0], cf_ref[0], btf_ref[0], cumf_ref[0], cumtf_ref[0],
                               hf_ref, 0, True)
    yb_ref[0] = _ssd_direction(xb_ref[0], cb_ref[0], btb_ref[0], cumb_ref[0], cumtb_ref[0],
                               hb_ref, SSD_HEADS, False)


def _ssd_scan(xdtf, xdtb, cm, bt, cum, cumt, n_ctx):
    nb, t, _ = xdtf.shape
    nc = t // SSD_CHUNK
    ncc = n_ctx // SSD_CHUNK
    ctr = cumt.shape[1]

    def bmap(s):
        return jnp.where(s < ncc, ncc - 1 - s, nc - 1 - (s - ncc))

    fmap = lambda s: s
    rows = lambda w, cm_: pl.BlockSpec((1, SSD_CHUNK, w), lambda b, s: (b, cm_(s), 0))
    cols = lambda h, cm_: pl.BlockSpec((1, h, SSD_CHUNK), lambda b, s: (b, 0, cm_(s)))
    side = lambda cm_: [rows(SSD_WIDTH, cm_), rows(SSD_BC, cm_), cols(SSD_BC, cm_),
                        rows(DT_PAD, cm_), cols(ctr, cm_)]
    return pl.pallas_call(
        _ssd_scan_kernel,
        out_shape=(jax.ShapeDtypeStruct((nb, t, SSD_WIDTH), F32),
                   jax.ShapeDtypeStruct((nb, t, SSD_WIDTH), F32)),
        grid=(nb, nc),
        in_specs=side(fmap) + side(bmap),
        out_specs=(rows(SSD_WIDTH, fmap), rows(SSD_WIDTH, bmap)),
        scratch_shapes=[pltpu.VMEM((SSD_STATE, SSD_WIDTH), F32), pltpu.VMEM((SSD_STATE, SSD_WIDTH), F32)],
        compiler_params=_params(("arbitrary", "arbitrary")),
        name="ssd_scan",
    )(xdtf, cm, bt, cum, cumt, xdtb, cm, bt, cum, cumt)


def _head_norm(x, bd, g):
    sq = x * x
    hi = sq.astype(BF16)
    lo = (sq - hi.astype(F32)).astype(BF16)
    parts = []
    w = bd.shape[0]
    for s in range(x.shape[1] // w):
        sl = slice(s * w, (s + 1) * w)
        parts.append(_dot(hi[:, sl], bd) + _dot(lo[:, sl], bd))
    ss = jnp.concatenate(parts, axis=1) if len(parts) > 1 else parts[0]
    return x * lax.rsqrt(ss * (1.0 / HEAD_DIM) + EPS) * g


def _rope(x, cs, sn):
    w = x.shape[1]
    lane = lax.broadcasted_iota(jnp.int32, x.shape, 1)
    first_half = (lane % (2 * ROPE_NF)) < ROPE_NF
    partner = jnp.where(first_half, pltpu.roll(x, w - ROPE_NF, axis=1), pltpu.roll(x, ROPE_NF, axis=1))
    return x * cs + partner * sn


def _attn_prep_kernel(q_ref, k_ref, v_ref, cs_ref, sn_ref, bd_ref, qg_ref, kg_ref,
                      qt_ref, ko_ref, vt_ref):
    bd = bd_ref[...]
    q = _rope(_head_norm(q_ref[0], bd, qg_ref[...]), cs_ref[...], sn_ref[...])
    q = q * (HEAD_DIM ** -0.5 * math.log2(math.e))
    qt_ref[0] = q.T.astype(BF16)
    k = _rope(_head_norm(k_ref[0], bd, kg_ref[...]), cs_ref[:, :KV_WIDTH], sn_ref[:, :KV_WIDTH])
    ko_ref[0, 0] = k.astype(BF16)
    vt_ref[0, 0] = v_ref[0].T.astype(BF16)


def _attn_prep(q, k, v, cs, sn, bd, qg, kg):
    nb, t, _ = q.shape
    nt = t // TILE
    tile = lambda w: pl.BlockSpec((1, TILE, w), lambda b, i: (b, i, 0))
    const = lambda shp: pl.BlockSpec(shp, lambda b, i: tuple(0 for _ in shp))
    return pl.pallas_call(
        _attn_prep_kernel,
        out_shape=(jax.ShapeDtypeStruct((nb, ATTN_WIDTH, t), BF16),
                   jax.ShapeDtypeStruct((nb, nt, TILE, KV_WIDTH), BF16),
                   jax.ShapeDtypeStruct((nb, nt, KV_WIDTH, TILE), BF16)),
        grid=(nb, nt),
        in_specs=[tile(ATTN_WIDTH), tile(KV_WIDTH), tile(KV_WIDTH),
                  pl.BlockSpec((TILE, ATTN_WIDTH), lambda b, i: (i, 0)),
                  pl.BlockSpec((TILE, ATTN_WIDTH), lambda b, i: (i, 0)),
                  const(bd.shape), const((1, ATTN_WIDTH)), const((1, KV_WIDTH))],
        out_specs=(pl.BlockSpec((1, ATTN_WIDTH, TILE), lambda b, i: (b, 0, i)),
                   pl.BlockSpec((1, 1, TILE, KV_WIDTH), lambda b, i: (b, i, 0, 0)),
                   pl.BlockSpec((1, 1, KV_WIDTH, TILE), lambda b, i: (b, i, 0, 0))),
        compiler_params=_params(("arbitrary", "arbitrary")),
        name="attn_prep",
    )(q, k, v, cs, sn, bd, qg, kg)


def _attn_kernel(qt_ref, k_ref, vt_ref, ga_ref, o_ref, ot_ref):
    i = pl.program_id(1)
    n_chunks = k_ref.shape[1]
    nk = jnp.where(i == 0, 1, n_chunks)
    gq = ATTN_REP * HEAD_DIM
    wq = ATTN_REP * TILE
    for g in range(ATTN_KV_HEADS):
        lane_tile = (g * HEAD_DIM) // LANES
        upper = (g * HEAD_DIM) % LANES != 0
        zero = jnp.zeros((HEAD_DIM, TILE), BF16)
        blocks = []
        for r in range(ATTN_REP):
            qh = qt_ref[0, g * gq + r * HEAD_DIM:g * gq + (r + 1) * HEAD_DIM, :]
            blocks.append(jnp.concatenate([zero, qh] if upper else [qh, zero], axis=0))
        qpad = jnp.concatenate(blocks, axis=1)

        def body(c, carry, g=g, lane_tile=lane_tile, qpad=qpad):
            m, l, acc = carry
            kc = k_ref[0, c, :, lane_tile * LANES:(lane_tile + 1) * LANES]
            s = _dot(kc, qpad)
            m_new = jnp.maximum(m, jnp.max(s, axis=0, keepdims=True))
            alpha = jnp.exp2(m - m_new)
            p = jnp.exp2(s - m_new)
            l = alpha * l + jnp.sum(p, axis=0, keepdims=True)
            vc = vt_ref[0, c, g * HEAD_DIM:(g + 1) * HEAD_DIM, :]
            acc = alpha * acc + _dot(vc, p.astype(BF16))
            return m_new, l, acc

        init = (jnp.full((1, wq), -jnp.inf, F32), jnp.zeros((1, wq), F32),
                jnp.zeros((HEAD_DIM, wq), F32))
        m, l, acc = lax.fori_loop(0, nk, body, init)
        ot = acc / l
        for r in range(ATTN_REP):
            ot_ref[g * gq + r * HEAD_DIM:g * gq + (r + 1) * HEAD_DIM, :] = ot[:, r * TILE:(r + 1) * TILE]
    o_ref[0] = ot_ref[...].T * _silu(ga_ref[0])


def _attention(qt, kk, vt, ga):
    nb, t, _ = ga.shape
    nt = t // TILE
    return pl.pallas_call(
        _attn_kernel,
        out_shape=jax.ShapeDtypeStruct((nb, t, ATTN_WIDTH), F32),
        grid=(nb, nt),
        in_specs=[pl.BlockSpec((1, ATTN_WIDTH, TILE), lambda b, i: (b, 0, i)),
                  pl.BlockSpec((1, nt, TILE, KV_WIDTH), lambda b, i: (b, 0, 0, 0)),
                  pl.BlockSpec((1, nt, KV_WIDTH, TILE), lambda b, i: (b, 0, 0, 0)),
                  pl.BlockSpec((1, TILE, ATTN_WIDTH), lambda b, i: (b, i, 0))],
        out_specs=pl.BlockSpec((1, TILE, ATTN_WIDTH), lambda b, i: (b, i, 0)),
        scratch_shapes=[pltpu.VMEM((ATTN_WIDTH, TILE), F32)],
        compiler_params=_params(("arbitrary", "arbitrary")),
        name="attention",
    )(qt, kk, vt, ga)


def _glu(a, b):
    return a * jax.nn.sigmoid(b)


def _conformer_kernel(ua_ref, ub_ref, uap_ref, ubp_ref, uan_ref, ubn_ref, gc_ref,
                      cw_ref, cb_ref, lg_ref, lb_ref, pw_ref, pb_ref, o_ref, ext_ref):
    i = pl.program_id(1)
    nt = pl.num_programs(1)
    prev_ok = i >= 2
    next_ok = jnp.logical_and(i != 0, i != nt - 1)
    ext_ref[0:CM_HALO, :] = jnp.where(prev_ok, _glu(uap_ref[0], ubp_ref[0]), 0.0)
    ext_ref[CM_HALO:CM_HALO + TILE, :] = _glu(ua_ref[0], ub_ref[0])
    ext_ref[CM_HALO + TILE:, :] = jnp.where(next_ok, _glu(uan_ref[0], ubn_ref[0]), 0.0)
    pad = CM_CONV_LEN // 2
    acc = jnp.broadcast_to(cb_ref[...], (TILE, CM_WIDTH))
    for k in range(CM_CONV_LEN):
        acc = acc + ext_ref[pl.ds(CM_HALO - pad + k, TILE), :] * cw_ref[k:k + 1, :]
    mu = jnp.mean(acc, axis=-1, keepdims=True)
    xc = acc - mu
    var = jnp.mean(xc * xc, axis=-1, keepdims=True)
    u = _silu(xc * lax.rsqrt(var + EPS) * lg_ref[...] + lb_ref[...])
    u = _dot(u.astype(BF16), pw_ref[...]) + pb_ref[...]
    o_ref[0] = u * _silu(gc_ref[0])


def _conformer(ua, ub, gc, conv_w, conv_b, ln_g, ln_b, pw_w, pw_b):
    nb, t, _ = ua.shape
    nt = t // TILE
    hb = TILE // CM_HALO
    nhb = t // CM_HALO
    cw = jnp.zeros((32, CM_WIDTH), F32).at[:CM_CONV_LEN].set(conv_w)
    tile = pl.BlockSpec((1, TILE, CM_WIDTH), lambda b, i: (b, i, 0))
    prev = pl.BlockSpec((1, CM_HALO, CM_WIDTH), lambda b, i: (b, jnp.maximum(i * hb - 1, 0), 0))
    nxt = pl.BlockSpec((1, CM_HALO, CM_WIDTH), lambda b, i: (b, jnp.minimum((i + 1) * hb, nhb - 1), 0))
    const = lambda shp: pl.BlockSpec(shp, lambda b, i: tuple(0 for _ in shp))
    row = lambda v: v.reshape(1, CM_WIDTH)
    return pl.pallas_call(
        _conformer_kernel,
        out_shape=jax.ShapeDtypeStruct((nb, t, CM_WIDTH), F32),
        grid=(nb, nt),
        in_specs=[tile, tile, prev, prev, nxt, nxt, tile,
                  const((32, CM_WIDTH)), const((1, CM_WIDTH)), const((1, CM_WIDTH)), const((1, CM_WIDTH)),
                  const((CM_WIDTH, CM_WIDTH)), const((1, CM_WIDTH))],
        out_specs=tile,
        scratch_shapes=[pltpu.VMEM((TILE + 2 * CM_HALO, CM_WIDTH), F32)],
        compiler_params=_params(("arbitrary", "arbitrary")),
        name="conformer",
    )(ua, ub, ua, ub, ua, ub, gc, cw, row(conv_b), row(ln_g), row(ln_b), pw_w.astype(BF16), row(pw_b))


def _outproj_kernel(x_ref, yf_ref, yb_ref, xs_ref, z_ref, at_ref, cv_ref, mod_ref,
                    dsk_ref, ng_ref, w_ref, fg_ref, o_ref, *, tile_off, final):
    d = x_ref.shape[-1]
    b = pl.program_id(0)
    i = pl.program_id(1) + tile_off
    row = jnp.where(i == 0, CTX_ROW, b)
    gate = mod_ref[pl.ds(row, 1), :][:, 2 * d:]
    y = yf_ref[0] + yb_ref[0] + dsk_ref[...] * xs_ref[0].astype(F32)
    y = y * _silu(z_ref[0])
    y = y * lax.rsqrt(jnp.mean(y * y, axis=-1, keepdims=True) + EPS) * ng_ref[...]
    cat = jnp.concatenate([y.astype(BF16), at_ref[0].astype(BF16), cv_ref[0].astype(BF16)], axis=1)
    xn = x_ref[0] + gate * _dot(cat, w_ref[...])
    if final:
        xn = xn * lax.rsqrt(jnp.mean(xn * xn, axis=-1, keepdims=True) + EPS) * fg_ref[...]
    o_ref[0] = xn


def _outproj(xall, yf, yb, xs, z, at, cv, mod_l, d_skip_exp, norm_g, w_out, final_g, n_ctx, final):
    nb, t, d = xall.shape
    tile_off = n_ctx // TILE if final else 0
    nt = t // TILE - tile_off
    tile = lambda w: pl.BlockSpec((1, TILE, w), lambda b, i: (b, i + tile_off, 0))
    const = lambda shp: pl.BlockSpec(shp, lambda b, i: tuple(0 for _ in shp))
    return pl.pallas_call(
        functools.partial(_outproj_kernel, tile_off=tile_off, final=final),
        out_shape=jax.ShapeDtypeStruct((nb, nt * TILE, d), F32),
        grid=(nb, nt),
        in_specs=[tile(d), tile(SSD_WIDTH), tile(SSD_WIDTH), tile(SSD_WIDTH), tile(SSD_WIDTH),
                  tile(ATTN_WIDTH), tile(CM_WIDTH), const(mod_l.shape),
                  const((1, SSD_WIDTH)), const((1, SSD_WIDTH)), const(w_out.shape), const((1, d))],
        out_specs=pl.BlockSpec((1, TILE, d), lambda b, i: (b, i, 0)),
        compiler_params=_params(("arbitrary", "arbitrary")),
        name="out_proj_final" if final else "out_proj",
    )(xall, yf, yb, xs, z, at, cv, mod_l, d_skip_exp.reshape(1, SSD_WIDTH),
      norm_g.reshape(1, SSD_WIDTH), w_out, final_g.reshape(1, d))


def _expand_matrix():
    nh2 = 2 * SSD_HEADS
    e = np.zeros((DT_PAD, 2 * SSD_WIDTH), np.float32)
    for src in range(nh2):
        for term in range(3):
            e[term * nh2 + src, src * SSD_HEAD_DIM:(src + 1) * SSD_HEAD_DIM] = 1.0
    return jnp.asarray(e, BF16)


def _block_ones(width):
    idx = np.arange(width) // HEAD_DIM
    return jnp.asarray((idx[:, None] == idx[None, :]).astype(np.float32), BF16)


def _rope_tables(seq, n_ctx):
    n_rows = seq // GRID_W
    rows = jnp.repeat(jnp.arange(n_rows, dtype=F32), GRID_W)
    cols = jnp.tile(jnp.arange(GRID_W, dtype=F32), n_rows)
    inv = ROPE_BASE ** (-jnp.arange(ROPE_NF, dtype=F32) / ROPE_NF)
    ar, ac = rows[:, None] * inv, cols[:, None] * inv
    cs = jnp.concatenate([jnp.cos(ar), jnp.cos(ar), jnp.cos(ac), jnp.cos(ac)], axis=1)
    sn = jnp.concatenate([-jnp.sin(ar), jnp.sin(ar), -jnp.sin(ac), jnp.sin(ac)], axis=1)
    cs = jnp.concatenate([jnp.ones((n_ctx, HEAD_DIM), F32), cs], axis=0)
    sn = jnp.concatenate([jnp.zeros((n_ctx, HEAD_DIM), F32), sn], axis=0)
    return jnp.tile(cs, (1, ATTN_HEADS)), jnp.tile(sn, (1, ATTN_HEADS))


def kernel(x, c, ctx, c_ctx, norm_g, ada_w, ada_b, w_in, ssd_conv_w, ssd_conv_b, ssd_dt_bias, ssd_a_log, ssd_d, ssd_norm_g, q_norm_g, k_norm_g, cm_conv_w, cm_conv_b, cm_ln_g, cm_ln_b, cm_pw_w, cm_pw_b, w_out, final_norm_g):
    nb, seq, d = x.shape
    n_ctx = ctx.shape[1]
    depth = w_in.shape[0]
    assert n_ctx == TILE and seq % TILE == 0 and seq % GRID_W == 0 and nb <= CTX_ROW
    nh2 = 2 * SSD_HEADS
    z_end = SSD_WIDTH + SSD_XBC

    xall = jnp.concatenate([ctx, x], axis=1)
    cvec = jnp.zeros((MOD_ROWS, d), F32).at[:nb].set(c).at[CTX_ROW].set(c_ctx)
    mod = _ada(cvec, ada_w, ada_b)
    cs, sn = _rope_tables(seq, n_ctx)
    e3 = _expand_matrix()
    bd = _block_ones(2 * LANES)

    for l in range(depth):
        final = l == depth - 1
        w = w_in[l]
        w_all = jnp.concatenate([w[:, :z_end], w[:, z_end + nh2:], w[:, z_end:z_end + nh2],
                                 jnp.zeros((d, DT_PAD - nh2), F32)], axis=1).astype(BF16)
        z, xbc, q, k, v, ga, ua, ub, gc, dtr = _inproj(xall, mod[l], norm_g[l], w_all)
        xs, xdtf, xdtb, bm, cm, bt, cum, cumt = _ssd_prep(
            xbc, dtr, ssd_conv_w[l], ssd_conv_b[l], ssd_dt_bias[l], ssd_a_log[l], e3)
        del bm
        yf, yb = _ssd_scan(xdtf, xdtb, cm, bt, cum, cumt, n_ctx)
        qt, kk, vt = _attn_prep(q, k, v, cs, sn, bd,
                                jnp.tile(q_norm_g[l], ATTN_HEADS).reshape(1, ATTN_WIDTH),
                                jnp.tile(k_norm_g[l], ATTN_KV_HEADS).reshape(1, KV_WIDTH))
        at = _attention(qt, kk, vt, ga)
        cv = _conformer(ua, ub, gc, cm_conv_w[l], cm_conv_b[l], cm_ln_g[l], cm_ln_b[l],
                        cm_pw_w[l], cm_pw_b[l])
        xall = _outproj(xall, yf, yb, xs, z, at, cv, mod[l], jnp.repeat(ssd_d[l], SSD_HEAD_DIM),
                        ssd_norm_g[l], w_out[l].astype(BF16), final_norm_g, n_ctx, final)
    return xall
```

```python
import functools
import math

import numpy as np
import jax
import jax.numpy as jnp
from jax import lax
from jax.experimental import pallas as pl
from jax.experimental.pallas import tpu as pltpu

F32 = jnp.float32
BF16 = jnp.bfloat16

EPS = 1e-6
GRID_W = 64
ROPE_BASE = 10000.0

SSD_HEADS = 12
SSD_HEAD_DIM = 64
SSD_WIDTH = SSD_HEADS * SSD_HEAD_DIM
SSD_GROUPS = 2
SSD_STATE = 128
SSD_BC = SSD_GROUPS * SSD_STATE
SSD_XBC = SSD_WIDTH + 2 * SSD_BC
SSD_CONV_LEN = 5
SSD_CHUNK = 128
ATTN_HEADS = 12
ATTN_KV_HEADS = 4
ATTN_REP = ATTN_HEADS // ATTN_KV_HEADS
HEAD_DIM = 64
ATTN_WIDTH = ATTN_HEADS * HEAD_DIM
KV_WIDTH = ATTN_KV_HEADS * HEAD_DIM
ROPE_NF = HEAD_DIM // 4
V_AUG = HEAD_DIM + 16
CM_WIDTH = 512
CM_CONV_LEN = 31
MIX_WIDTH = SSD_WIDTH + ATTN_WIDTH + CM_WIDTH

LANES = 128
DT_PAD = LANES
TILE = 256
SSD_HALO = 8
CM_HALO = 16
CTX_ROW = 4
MOD_ROWS = 8
VMEM_LIMIT = 56 * 1024 * 1024

PROJ_WIDTHS = (SSD_WIDTH, SSD_XBC, ATTN_WIDTH, KV_WIDTH, KV_WIDTH, ATTN_WIDTH,
               CM_WIDTH, CM_WIDTH, CM_WIDTH, DT_PAD)
PROJ_TOTAL = sum(PROJ_WIDTHS)


def _dot(a, b):
    return jnp.dot(a, b, preferred_element_type=F32)


def _silu(x):
    return x * jax.nn.sigmoid(x)


def _split3(x):
    hi = x.astype(BF16)
    r1 = x - hi.astype(F32)
    mid = r1.astype(BF16)
    lo = (r1 - mid.astype(F32)).astype(BF16)
    return hi, mid, lo


def _params(sem):
    return pltpu.CompilerParams(dimension_semantics=sem, vmem_limit_bytes=VMEM_LIMIT)


def _ada_kernel(c_ref, w_ref, b_ref, o_ref):
    s = _silu(c_ref[...])
    s_hi = s.astype(BF16)
    s_lo = (s - s_hi.astype(F32)).astype(BF16)
    w = w_ref[0]
    w_hi = w.astype(BF16)
    w_lo = (w - w_hi.astype(F32)).astype(BF16)
    o_ref[0] = _dot(s_hi, w_hi) + _dot(s_hi, w_lo) + _dot(s_lo, w_hi) + b_ref[0]


def _ada(cvec, ada_w, ada_b):
    depth, d, n3 = ada_w.shape
    tn = 1024
    return pl.pallas_call(
        _ada_kernel,
        out_shape=jax.ShapeDtypeStruct((depth, MOD_ROWS, n3), F32),
        grid=(depth, n3 // tn),
        in_specs=[
            pl.BlockSpec((MOD_ROWS, d), lambda l, j: (0, 0)),
            pl.BlockSpec((1, d, tn), lambda l, j: (l, 0, j)),
            pl.BlockSpec((1, 1, tn), lambda l, j: (l, 0, j)),
        ],
        out_specs=pl.BlockSpec((1, MOD_ROWS, tn), lambda l, j: (l, 0, j)),
        compiler_params=_params(("arbitrary", "arbitrary")),
        name="ada_mod",
    )(cvec, ada_w, ada_b.reshape(depth, 1, n3))


def _inproj_kernel(x_ref, mod_ref, g_ref, w_ref, *out_refs):
    d = x_ref.shape[-1]
    b = pl.program_id(0)
    i = pl.program_id(1)
    row = jnp.where(i == 0, CTX_ROW, b)
    m = mod_ref[pl.ds(row, 1), :]
    shift = m[:, :d]
    scale = m[:, d:2 * d]
    x = x_ref[0]
    ms = jnp.mean(x * x, axis=-1, keepdims=True)
    h = (x * lax.rsqrt(ms + EPS) * g_ref[...]) * (1.0 + scale) + shift
    hb = h.astype(BF16)
    off = 0
    for o_ref, wd in zip(out_refs, PROJ_WIDTHS):
        o_ref[0] = _dot(hb, w_ref[:, off:off + wd])
        off += wd


def _inproj(xall, mod_l, g, w_all):
    nb, t, d = xall.shape
    nt = t // TILE
    outs = tuple(jax.ShapeDtypeStruct((nb, t, wd), F32) for wd in PROJ_WIDTHS)
    return pl.pallas_call(
        _inproj_kernel,
        out_shape=outs,
        grid=(nb, nt),
        in_specs=[
            pl.BlockSpec((1, TILE, d), lambda b, i: (b, i, 0)),
            pl.BlockSpec(mod_l.shape, lambda b, i: (0, 0)),
            pl.BlockSpec((1, d), lambda b, i: (0, 0)),
            pl.BlockSpec(w_all.shape, lambda b, i: (0, 0)),
        ],
        out_specs=tuple(pl.BlockSpec((1, TILE, wd), lambda b, i: (b, i, 0)) for wd in PROJ_WIDTHS),
        compiler_params=_params(("arbitrary", "arbitrary")),
        name="in_proj",
    )(xall, mod_l, g.reshape(1, d), w_all)


def _ssd_prep_kernel(xc_ref, xp_ref, xn_ref, dt_ref, cw_ref, cb_ref, dtb_ref, alog_ref, e3_ref,
                     xs_ref, xdtf_ref, xdtb_ref, bm_ref, cm_ref, bt_ref, cum_ref, cumt_ref,
                     ext_ref):
    i = pl.program_id(1)
    nt = pl.num_programs(1)
    prev_ok = i >= 2
    next_ok = jnp.logical_and(i != 0, i != nt - 1)
    ext_ref[0:SSD_HALO, :] = jnp.where(prev_ok, xp_ref[0], 0.0)
    ext_ref[SSD_HALO:SSD_HALO + TILE, :] = xc_ref[0]
    ext_ref[SSD_HALO + TILE:, :] = jnp.where(next_ok, xn_ref[0], 0.0)
    pad = SSD_CONV_LEN // 2
    acc = jnp.broadcast_to(cb_ref[...], (TILE, SSD_XBC))
    for k in range(SSD_CONV_LEN):
        acc = acc + ext_ref[pl.ds(SSD_HALO - pad + k, TILE), :] * cw_ref[k:k + 1, :]
    y = _silu(acc)
    xs = y[:, :SSD_WIDTH]
    bm = y[:, SSD_WIDTH:SSD_WIDTH + SSD_BC]
    xs_ref[0] = xs.astype(BF16)
    bm_ref[0] = bm.astype(BF16)
    cm_ref[0] = y[:, SSD_WIDTH + SSD_BC:].astype(BF16)
    bt_ref[0] = bm.T.astype(BF16)

    lane = lax.broadcasted_iota(jnp.int32, (TILE, DT_PAD), 1)
    nh2 = 2 * SSD_HEADS
    z = dt_ref[0] + dtb_ref[...]
    dt = jnp.maximum(z, 0.0) + jnp.log1p(jnp.exp(-jnp.abs(z)))
    dt = jnp.where(lane < nh2, dt, 0.0)
    a = dt * (-jnp.exp(alog_ref[...]))
    a = jnp.where(lane < nh2, a, 0.0)

    d_hi, d_mid, d_lo = _split3(dt)
    d3 = (d_hi.astype(F32) + pltpu.roll(d_mid.astype(F32), nh2, axis=1)
          + pltpu.roll(d_lo.astype(F32), 2 * nh2, axis=1)).astype(BF16)
    dexp = _dot(d3, e3_ref[...])
    xdtf_ref[0] = (xs * dexp[:, :SSD_WIDTH]).astype(BF16)
    xdtb_ref[0] = (xs * dexp[:, SSD_WIDTH:]).astype(BF16)

    r = lax.broadcasted_iota(jnp.int32, (SSD_CHUNK, 3 * SSD_CHUNK), 0)
    c = lax.broadcasted_iota(jnp.int32, (SSD_CHUNK, 3 * SSD_CHUNK), 1) % SSD_CHUNK
    tri_l = jnp.where(c <= r, 1.0, 0.0).astype(BF16)
    tri_u = jnp.where(c >= r, 1.0, 0.0).astype(BF16)
    lane_c = lax.broadcasted_iota(jnp.int32, (SSD_CHUNK, DT_PAD), 1)
    cums = []
    for ch in range(TILE // SSD_CHUNK):
        a_c = a[ch * SSD_CHUNK:(ch + 1) * SSD_CHUNK, :]
        a3 = jnp.concatenate(_split3(a_c), axis=0)
        cums.append(jnp.where(lane_c < SSD_HEADS, _dot(tri_l, a3), _dot(tri_u, a3)))
    cum = jnp.concatenate(cums, axis=0)
    cum_ref[0] = cum
    cumt_ref[0] = cum.T[:cumt_ref.shape[1], :]


def _ssd_prep(xbc, dt_raw, conv_w, conv_b, dt_bias, a_log, e3):
    nb, t, _ = xbc.shape
    nt = t // TILE
    hb = TILE // SSD_HALO
    nhb = t // SSD_HALO
    cw = jnp.zeros((8, SSD_XBC), F32).at[:SSD_CONV_LEN].set(conv_w)
    pad1 = lambda v: jnp.zeros((1, DT_PAD), F32).at[0, :v.shape[0]].set(v)
    ctr = 32
    tok = lambda w, dt_: jax.ShapeDtypeStruct((nb, t, w), dt_)
    tile = lambda w: pl.BlockSpec((1, TILE, w), lambda b, i: (b, i, 0))
    const = lambda shp: pl.BlockSpec(shp, lambda b, i: tuple(0 for _ in shp))
    return pl.pallas_call(
        _ssd_prep_kernel,
        out_shape=(tok(SSD_WIDTH, BF16), tok(SSD_WIDTH, BF16), tok(SSD_WIDTH, BF16),
                   tok(SSD_BC, BF16), tok(SSD_BC, BF16),
                   jax.ShapeDtypeStruct((nb, SSD_BC, t), BF16),
                   tok(DT_PAD, F32),
                   jax.ShapeDtypeStruct((nb, ctr, t), F32)),
        grid=(nb, nt),
        in_specs=[
            tile(SSD_XBC),
            pl.BlockSpec((1, SSD_HALO, SSD_XBC), lambda b, i: (b, jnp.maximum(i * hb - 1, 0), 0)),
            pl.BlockSpec((1, SSD_HALO, SSD_XBC), lambda b, i: (b, jnp.minimum((i + 1) * hb, nhb - 1), 0)),
            tile(DT_PAD),
            const((8, SSD_XBC)), const((1, SSD_XBC)), const((1, DT_PAD)), const((1, DT_PAD)),
            const(e3.shape),
        ],
        out_specs=(tile(SSD_WIDTH), tile(SSD_WIDTH), tile(SSD_WIDTH), tile(SSD_BC), tile(SSD_BC),
                   pl.BlockSpec((1, SSD_BC, TILE), lambda b, i: (b, 0, i)),
                   tile(DT_PAD),
                   pl.BlockSpec((1, ctr, TILE), lambda b, i: (b, 0, i))),
        scratch_shapes=[pltpu.VMEM((TILE + 2 * SSD_HALO, SSD_XBC), F32)],
        compiler_params=_params(("arbitrary", "arbitrary")),
        name="ssd_prep",
    )(xbc, xbc, xbc, dt_raw, cw, conv_b.reshape(1, SSD_XBC), pad1(dt_bias), pad1(a_log), e3)


def _ssd_direction(xdt, cm, bt, cum, cumt, h_ref, col0, fwd):
    n = SSD_CHUNK
    li = lax.broadcasted_iota(jnp.int32, (n, n), 0)
    si = lax.broadcasted_iota(jnp.int32, (n, n), 1)
    mask = (li >= si) if fwd else (si >= li)
    lo_half = si < SSD_HEAD_DIM
    tot_row = n - 1 if fwd else 0
    hprev = h_ref[...]
    hb = hprev.astype(BF16)
    gw = SSD_WIDTH // SSD_GROUPS
    pairs = SSD_HEADS // SSD_GROUPS // 2
    y_slabs = []
    h_new = []
    for g in range(SSD_GROUPS):
        cg = cm[:, g * SSD_STATE:(g + 1) * SSD_STATE]
        btg = bt[g * SSD_STATE:(g + 1) * SSD_STATE, :]
        cb = _dot(cg, btg)
        yoff = _dot(cg, hb[:, g * gw:(g + 1) * gw])
        xw_slabs = []
        etot_slabs = []
        for jj in range(pairs):
            j = g * pairs + jj
            h0 = col0 + 2 * j
            cb0 = jnp.broadcast_to(cum[:, h0:h0 + 1], (n, n))
            cb1 = jnp.broadcast_to(cum[:, h0 + 1:h0 + 2], (n, n))
            l0 = jnp.exp(jnp.where(mask, cb0 - cumt[h0:h0 + 1, :], -jnp.inf))
            l1 = jnp.exp(jnp.where(mask, cb1 - cumt[h0 + 1:h0 + 2, :], -jnp.inf))
            mm = jnp.concatenate([(cb * l0).astype(BF16), (cb * l1).astype(BF16)], axis=1)
            slab = xdt[:, j * LANES:(j + 1) * LANES]
            zero = jnp.zeros_like(slab)
            ww = jnp.concatenate([jnp.where(lo_half, slab, zero), jnp.where(lo_half, zero, slab)], axis=0)
            ydiag = _dot(mm, ww)
            csel = jnp.where(lo_half, cb0, cb1)
            tot = csel[tot_row:tot_row + 1, :]
            y_slabs.append(ydiag + yoff[:, jj * LANES:(jj + 1) * LANES] * jnp.exp(csel))
            xw_slabs.append((slab.astype(F32) * jnp.exp(tot - csel)).astype(BF16))
            etot_slabs.append(jnp.exp(tot))
        xw = jnp.concatenate(xw_slabs, axis=1)
        etot = jnp.concatenate(etot_slabs, axis=1)
        h_new.append(hprev[:, g * gw:(g + 1) * gw] * etot + _dot(btg, xw))
    h_ref[...] = jnp.concatenate(h_new, axis=1)
    return jnp.concatenate(y_slabs, axis=1)


def _ssd_scan_kernel(xf_ref, cf_ref, btf_ref, cumf_ref, cumtf_ref,
                     xb_ref, cb_ref, btb_ref, cumb_ref, cumtb_ref,
                     yf_ref, yb_ref, hf_ref, hb_ref):
    @pl.when(pl.program_id(1) == 0)
    def _():
        hf_ref[...] = jnp.zeros_like(hf_ref)
        hb_ref[...] = jnp.zeros_like(hb_ref)

    yf_ref[0] = _ssd_direction(xf_ref[0], cf_ref[0], btf_ref[0], cumf_ref[0], cumtf_ref[0],
                               hf_ref, 0, True)
    yb_ref[0] = _ssd_direction(xb_ref[0], cb_ref[0], btb_ref[0], cumb_ref[0], cumtb_ref[0],
                               hb_ref, SSD_HEADS, False)


def _ssd_scan(xdtf, xdtb, cm, bt, cum, cumt, n_ctx):
    nb, t, _ = xdtf.shape
    nc = t // SSD_CHUNK
    ncc = n_ctx // SSD_CHUNK
    ctr = cumt.shape[1]

    def bmap(s):
        return jnp.where(s < ncc, ncc - 1 - s, nc - 1 - (s - ncc))

    fmap = lambda s: s
    rows = lambda w, cm_: pl.BlockSpec((1, SSD_CHUNK, w), lambda b, s: (b, cm_(s), 0))
    cols = lambda h, cm_: pl.BlockSpec((1, h, SSD_CHUNK), lambda b, s: (b, 0, cm_(s)))
    side = lambda cm_: [rows(SSD_WIDTH, cm_), rows(SSD_BC, cm_), cols(SSD_BC, cm_),
                        rows(DT_PAD, cm_), cols(ctr, cm_)]
    return pl.pallas_call(
        _ssd_scan_kernel,
        out_shape=(jax.ShapeDtypeStruct((nb, t, SSD_WIDTH), F32),
                   jax.ShapeDtypeStruct((nb, t, SSD_WIDTH), F32)),
        grid=(nb, nc),
        in_specs=side(fmap) + side(bmap),
        out_specs=(rows(SSD_WIDTH, fmap), rows(SSD_WIDTH, bmap)),
        scratch_shapes=[pltpu.VMEM((SSD_STATE, SSD_WIDTH), F32), pltpu.VMEM((SSD_STATE, SSD_WIDTH), F32)],
        compiler_params=_params(("arbitrary", "arbitrary")),
        name="ssd_scan",
    )(xdtf, cm, bt, cum, cumt, xdtb, cm, bt, cum, cumt)


def _head_norm(x, bd, g):
    sq = x * x
    hi = sq.astype(BF16)
    lo = (sq - hi.astype(F32)).astype(BF16)
    parts = []
    w = bd.shape[0]
    for s in range(x.shape[1] // w):
        sl = slice(s * w, (s + 1) * w)
        parts.append(_dot(hi[:, sl], bd) + _dot(lo[:, sl], bd))
    ss = jnp.concatenate(parts, axis=1) if len(parts) > 1 else parts[0]
    return x * lax.rsqrt(ss * (1.0 / HEAD_DIM) + EPS) * g


def _rope(x, cs, sn):
    w = x.shape[1]
    lane = lax.broadcasted_iota(jnp.int32, x.shape, 1)
    first_half = (lane % (2 * ROPE_NF)) < ROPE_NF
    partner = jnp.where(first_half, pltpu.roll(x, w - ROPE_NF, axis=1), pltpu.roll(x, ROPE_NF, axis=1))
    return x * cs + partner * sn


def _attn_prep_kernel(q_ref, k_ref, v_ref, cs_ref, sn_ref, bd_ref, qg_ref, kg_ref,
                      qt_ref, ko_ref, vt_ref):
    bd = bd_ref[...]
    q = _rope(_head_norm(q_ref[0], bd, qg_ref[...]), cs_ref[...], sn_ref[...])
    q = q * (HEAD_DIM ** -0.5 * math.log2(math.e))
    qt_ref[0] = q.T.astype(BF16)
    k = _rope(_head_norm(k_ref[0], bd, kg_ref[...]), cs_ref[:, :KV_WIDTH], sn_ref[:, :KV_WIDTH])
    ko_ref[0, 0] = k.astype(BF16)
    vt = v_ref[0].T.astype(BF16)
    ones_blk = jnp.where(lax.broadcasted_iota(jnp.int32, (V_AUG - HEAD_DIM, TILE), 0) == 0,
                         1.0, 0.0).astype(BF16)
    parts = []
    for g in range(ATTN_KV_HEADS):
        parts += [vt[g * HEAD_DIM:(g + 1) * HEAD_DIM, :], ones_blk]
    vt_ref[0, 0] = jnp.concatenate(parts, axis=0)


def _attn_prep(q, k, v, cs, sn, bd, qg, kg):
    nb, t, _ = q.shape
    nt = t // TILE
    tile = lambda w: pl.BlockSpec((1, TILE, w), lambda b, i: (b, i, 0))
    const = lambda shp: pl.BlockSpec(shp, lambda b, i: tuple(0 for _ in shp))
    return pl.pallas_call(
        _attn_prep_kernel,
        out_shape=(jax.ShapeDtypeStruct((nb, ATTN_WIDTH, t), BF16),
                   jax.ShapeDtypeStruct((nb, nt, TILE, KV_WIDTH), BF16),
                   jax.ShapeDtypeStruct((nb, nt, ATTN_KV_HEADS * V_AUG, TILE), BF16)),
        grid=(nb, nt),
        in_specs=[tile(ATTN_WIDTH), tile(KV_WIDTH), tile(KV_WIDTH),
                  pl.BlockSpec((TILE, ATTN_WIDTH), lambda b, i: (i, 0)),
                  pl.BlockSpec((TILE, ATTN_WIDTH), lambda b, i: (i, 0)),
                  const(bd.shape), const((1, ATTN_WIDTH)), const((1, KV_WIDTH))],
        out_specs=(pl.BlockSpec((1, ATTN_WIDTH, TILE), lambda b, i: (b, 0, i)),
                   pl.BlockSpec((1, 1, TILE, KV_WIDTH), lambda b, i: (b, i, 0, 0)),
                   pl.BlockSpec((1, 1, ATTN_KV_HEADS * V_AUG, TILE), lambda b, i: (b, i, 0, 0))),
        compiler_params=_params(("arbitrary", "arbitrary")),
        name="attn_prep",
    )(q, k, v, cs, sn, bd, qg, kg)


def _attn_kernel(qt_ref, k_ref, vt_ref, ga_ref, o_ref, qp_ref, s_ref, p_ref, acc_ref, ot_ref):
    i = pl.program_id(1)
    n_chunks = k_ref.shape[1]
    nk = jnp.where(i == 0, 1, n_chunks)
    gq = ATTN_REP * HEAD_DIM
    wq = ATTN_REP * TILE
    groups = range(ATTN_KV_HEADS)
    zero = jnp.zeros((HEAD_DIM, TILE), BF16)
    for g in groups:
        upper = (g * HEAD_DIM) % LANES != 0
        for r in range(ATTN_REP):
            qh = qt_ref[0, g * gq + r * HEAD_DIM:g * gq + (r + 1) * HEAD_DIM, :]
            qp_ref[g, :, r * TILE:(r + 1) * TILE] = jnp.concatenate([zero, qh] if upper else [qh, zero], axis=0)
    p_ref[...] = jnp.zeros_like(p_ref)
    acc_ref[...] = jnp.zeros_like(acc_ref)

    def qk_stage(g, c):
        lane_tile = (g * HEAD_DIM) // LANES
        kc = k_ref[0, c, :, lane_tile * LANES:(lane_tile + 1) * LANES]
        s = _dot(kc, qp_ref[g])
        s_ref[g] = s
        return jnp.max(s, axis=0, keepdims=True)

    def pv_stage(g, c, a_prev):
        vc = vt_ref[0, c, g * V_AUG:(g + 1) * V_AUG, :]
        acc_ref[g] = acc_ref[g] * a_prev + _dot(vc, p_ref[g])

    def body(c, carry):
        ms, a_prevs, cmaxs = carry
        c_prev = jnp.maximum(c - 1, 0)
        c_next = jnp.minimum(c + 1, nk - 1)
        new_m, new_a, new_cmax = [], [], []
        for g in groups:
            pv_stage(g, c_prev, a_prevs[g])
            m_new = jnp.maximum(ms[g], cmaxs[g])
            p_ref[g] = jnp.exp2(s_ref[g] - m_new).astype(BF16)
            new_cmax.append(qk_stage(g, c_next))
            new_m.append(m_new)
            new_a.append(jnp.exp2(ms[g] - m_new))
        return tuple(new_m), tuple(new_a), tuple(new_cmax)

    init = (tuple(jnp.full((1, wq), -jnp.inf, F32) for _ in groups),
            tuple(jnp.ones((1, wq), F32) for _ in groups),
            tuple(qk_stage(g, 0) for g in groups))
    _, a_last, _ = lax.fori_loop(0, nk, body, init)
    for g in groups:
        pv_stage(g, nk - 1, a_last[g])
        acc = acc_ref[g]
        ot = acc[:HEAD_DIM, :] / acc[HEAD_DIM:HEAD_DIM + 1, :]
        for r in range(ATTN_REP):
            ot_ref[g * gq + r * HEAD_DIM:g * gq + (r + 1) * HEAD_DIM, :] = ot[:, r * TILE:(r + 1) * TILE]
    o_ref[0] = ot_ref[...].T * _silu(ga_ref[0])


def _attention(qt, kk, vt, ga):
    nb, t, _ = ga.shape
    nt = t // TILE
    return pl.pallas_call(
        _attn_kernel,
        out_shape=jax.ShapeDtypeStruct((nb, t, ATTN_WIDTH), F32),
        grid=(nb, nt),
        in_specs=[pl.BlockSpec((1, ATTN_WIDTH, TILE), lambda b, i: (b, 0, i)),
                  pl.BlockSpec((1, nt, TILE, KV_WIDTH), lambda b, i: (b, 0, 0, 0)),
                  pl.BlockSpec((1, nt, ATTN_KV_HEADS * V_AUG, TILE), lambda b, i: (b, 0, 0, 0)),
                  pl.BlockSpec((1, TILE, ATTN_WIDTH), lambda b, i: (b, i, 0))],
        out_specs=pl.BlockSpec((1, TILE, ATTN_WIDTH), lambda b, i: (b, i, 0)),
        scratch_shapes=[pltpu.VMEM((ATTN_KV_HEADS, LANES, ATTN_REP * TILE), BF16),
                        pltpu.VMEM((ATTN_KV_HEADS, TILE, ATTN_REP * TILE), F32),
                        pltpu.VMEM((ATTN_KV_HEADS, TILE, ATTN_REP * TILE), BF16),
                        pltpu.VMEM((ATTN_KV_HEADS, V_AUG, ATTN_REP * TILE), F32),
                        pltpu.VMEM((ATTN_WIDTH, TILE), F32)],
        compiler_params=_params(("arbitrary", "arbitrary")),
        name="attention",
    )(qt, kk, vt, ga)


def _glu(a, b):
    return a * jax.nn.sigmoid(b)


def _conformer_kernel(ua_ref, ub_ref, uap_ref, ubp_ref, uan_ref, ubn_ref, gc_ref,
                      cw_ref, cb_ref, lg_ref, lb_ref, pw_ref, pb_ref, o_ref, ext_ref):
    i = pl.program_id(1)
    nt = pl.num_programs(1)
    prev_ok = i >= 2
    next_ok = jnp.logical_and(i != 0, i != nt - 1)
    ext_ref[0:CM_HALO, :] = jnp.where(prev_ok, _glu(uap_ref[0], ubp_ref[0]), 0.0)
    ext_ref[CM_HALO:CM_HALO + TILE, :] = _glu(ua_ref[0], ub_ref[0])
    ext_ref[CM_HALO + TILE:, :] = jnp.where(next_ok, _glu(uan_ref[0], ubn_ref[0]), 0.0)
    pad = CM_CONV_LEN // 2
    acc = jnp.broadcast_to(cb_ref[...], (TILE, CM_WIDTH))
    for k in range(CM_CONV_LEN):
        acc = acc + ext_ref[pl.ds(CM_HALO - pad + k, TILE), :] * cw_ref[k:k + 1, :]
    mu = jnp.mean(acc, axis=-1, keepdims=True)
    xc = acc - mu
    var = jnp.mean(xc * xc, axis=-1, keepdims=True)
    u = _silu(xc * lax.rsqrt(var + EPS) * lg_ref[...] + lb_ref[...])
    u = _dot(u.astype(BF16), pw_ref[...]) + pb_ref[...]
    o_ref[0] = u * _silu(gc_ref[0])


def _conformer(ua, ub, gc, conv_w, conv_b, ln_g, ln_b, pw_w, pw_b):
    nb, t, _ = ua.shape
    nt = t // TILE
    hb = TILE // CM_HALO
    nhb = t // CM_HALO
    cw = jnp.zeros((32, CM_WIDTH), F32).at[:CM_CONV_LEN].set(conv_w)
    tile = pl.BlockSpec((1, TILE, CM_WIDTH), lambda b, i: (b, i, 0))
    prev = pl.BlockSpec((1, CM_HALO, CM_WIDTH), lambda b, i: (b, jnp.maximum(i * hb - 1, 0), 0))
    nxt = pl.BlockSpec((1, CM_HALO, CM_WIDTH), lambda b, i: (b, jnp.minimum((i + 1) * hb, nhb - 1), 0))
    const = lambda shp: pl.BlockSpec(shp, lambda b, i: tuple(0 for _ in shp))
    row = lambda v: v.reshape(1, CM_WIDTH)
    return pl.pallas_call(
        _conformer_kernel,
        out_shape=jax.ShapeDtypeStruct((nb, t, CM_WIDTH), F32),
        grid=(nb, nt),
        in_specs=[tile, tile, prev, prev, nxt, nxt, tile,
                  const((32, CM_WIDTH)), const((1, CM_WIDTH)), const((1, CM_WIDTH)), const((1, CM_WIDTH)),
                  const((CM_WIDTH, CM_WIDTH)), const((1, CM_WIDTH))],
        out_specs=tile,
        scratch_shapes=[pltpu.VMEM((TILE + 2 * CM_HALO, CM_WIDTH), F32)],
        compiler_params=_params(("arbitrary", "arbitrary")),
        name="conformer",
    )(ua, ub, ua, ub, ua, ub, gc, cw, row(conv_b), row(ln_g), row(ln_b), pw_w.astype(BF16), row(pw_b))


def _outproj_kernel(x_ref, yf_ref, yb_ref, xs_ref, z_ref, at_ref, cv_ref, mod_ref,
                    dsk_ref, ng_ref, w_ref, fg_ref, o_ref, *, tile_off, final):
    d = x_ref.shape[-1]
    b = pl.program_id(0)
    i = pl.program_id(1) + tile_off
    row = jnp.where(i == 0, CTX_ROW, b)
    gate = mod_ref[pl.ds(row, 1), :][:, 2 * d:]
    y = yf_ref[0] + yb_ref[0] + dsk_ref[...] * xs_ref[0].astype(F32)
    y = y * _silu(z_ref[0])
    y = y * lax.rsqrt(jnp.mean(y * y, axis=-1, keepdims=True) + EPS) * ng_ref[...]
    cat = jnp.concatenate([y.astype(BF16), at_ref[0].astype(BF16), cv_ref[0].astype(BF16)], axis=1)
    xn = x_ref[0] + gate * _dot(cat, w_ref[...])
    if final:
        xn = xn * lax.rsqrt(jnp.mean(xn * xn, axis=-1, keepdims=True) + EPS) * fg_ref[...]
    o_ref[0] = xn


def _outproj(xall, yf, yb, xs, z, at, cv, mod_l, d_skip_exp, norm_g, w_out, final_g, n_ctx, final):
    nb, t, d = xall.shape
    tile_off = n_ctx // TILE if final else 0
    nt = t // TILE - tile_off
    tile = lambda w: pl.BlockSpec((1, TILE, w), lambda b, i: (b, i + tile_off, 0))
    const = lambda shp: pl.BlockSpec(shp, lambda b, i: tuple(0 for _ in shp))
    return pl.pallas_call(
        functools.partial(_outproj_kernel, tile_off=tile_off, final=final),
        out_shape=jax.ShapeDtypeStruct((nb, nt * TILE, d), F32),
        grid=(nb, nt),
        in_specs=[tile(d), tile(SSD_WIDTH), tile(SSD_WIDTH), tile(SSD_WIDTH), tile(SSD_WIDTH),
                  tile(ATTN_WIDTH), tile(CM_WIDTH), const(mod_l.shape),
                  const((1, SSD_WIDTH)), const((1, SSD_WIDTH)), const(w_out.shape), const((1, d))],
        out_specs=pl.BlockSpec((1, TILE, d), lambda b, i: (b, i, 0)),
        compiler_params=_params(("arbitrary", "arbitrary")),
        name="out_proj_final" if final else "out_proj",
    )(xall, yf, yb, xs, z, at, cv, mod_l, d_skip_exp.reshape(1, SSD_WIDTH),
      norm_g.reshape(1, SSD_WIDTH), w_out, final_g.reshape(1, d))


def _expand_matrix():
    nh2 = 2 * SSD_HEADS
    e = np.zeros((DT_PAD, 2 * SSD_WIDTH), np.float32)
    for src in range(nh2):
        for term in range(3):
            e[term * nh2 + src, src * SSD_HEAD_DIM:(src + 1) * SSD_HEAD_DIM] = 1.0
    return jnp.asarray(e, BF16)


def _block_ones(width):
    idx = np.arange(width) // HEAD_DIM
    return jnp.asarray((idx[:, None] == idx[None, :]).astype(np.float32), BF16)


def _rope_tables(seq, n_ctx):
    n_rows = seq // GRID_W
    rows = jnp.repeat(jnp.arange(n_rows, dtype=F32), GRID_W)
    cols = jnp.tile(jnp.arange(GRID_W, dtype=F32), n_rows)
    inv = ROPE_BASE ** (-jnp.arange(ROPE_NF, dtype=F32) / ROPE_NF)
    ar, ac = rows[:, None] * inv, cols[:, None] * inv
    cs = jnp.concatenate([jnp.cos(ar), jnp.cos(ar), jnp.cos(ac), jnp.cos(ac)], axis=1)
    sn = jnp.concatenate([-jnp.sin(ar), jnp.sin(ar), -jnp.sin(ac), jnp.sin(ac)], axis=1)
    cs = jnp.concatenate([jnp.ones((n_ctx, HEAD_DIM), F32), cs], axis=0)
    sn = jnp.concatenate([jnp.zeros((n_ctx, HEAD_DIM), F32), sn], axis=0)
    return jnp.tile(cs, (1, ATTN_HEADS)), jnp.tile(sn, (1, ATTN_HEADS))


def kernel(x, c, ctx, c_ctx, norm_g, ada_w, ada_b, w_in, ssd_conv_w, ssd_conv_b, ssd_dt_bias, ssd_a_log, ssd_d, ssd_norm_g, q_norm_g, k_norm_g, cm_conv_w, cm_conv_b, cm_ln_g, cm_ln_b, cm_pw_w, cm_pw_b, w_out, final_norm_g):
    nb, seq, d = x.shape
    n_ctx = ctx.shape[1]
    depth = w_in.shape[0]
    assert n_ctx == TILE and seq % TILE == 0 and seq % GRID_W == 0 and nb <= CTX_ROW
    nh2 = 2 * SSD_HEADS
    z_end = SSD_WIDTH + SSD_XBC

    xall = jnp.concatenate([ctx, x], axis=1)
    cvec = jnp.zeros((MOD_ROWS, d), F32).at[:nb].set(c).at[CTX_ROW].set(c_ctx)
    mod = _ada(cvec, ada_w, ada_b)
    cs, sn = _rope_tables(seq, n_ctx)
    e3 = _expand_matrix()
    bd = _block_ones(2 * LANES)

    for l in range(depth):
        final = l == depth - 1
        w = w_in[l]
        w_all = jnp.concatenate([w[:, :z_end], w[:, z_end + nh2:], w[:, z_end:z_end + nh2],
                                 jnp.zeros((d, DT_PAD - nh2), F32)], axis=1).astype(BF16)
        z, xbc, q, k, v, ga, ua, ub, gc, dtr = _inproj(xall, mod[l], norm_g[l], w_all)
        xs, xdtf, xdtb, bm, cm, bt, cum, cumt = _ssd_prep(
            xbc, dtr, ssd_conv_w[l], ssd_conv_b[l], ssd_dt_bias[l], ssd_a_log[l], e3)
        del bm
        yf, yb = _ssd_scan(xdtf, xdtb, cm, bt, cum, cumt, n_ctx)
        qt, kk, vt = _attn_prep(q, k, v, cs, sn, bd,
                                jnp.tile(q_norm_g[l], ATTN_HEADS).reshape(1, ATTN_WIDTH),
                                jnp.tile(k_norm_g[l], ATTN_KV_HEADS).reshape(1, KV_WIDTH))
        at = _attention(qt, kk, vt, ga)
        cv = _conformer(ua, ub, gc, cm_conv_w[l], cm_conv_b[l], cm_ln_g[l], cm_ln_b[l],
                        cm_pw_w[l], cm_pw_b[l])
        xall = _outproj(xall, yf, yb, xs, z, at, cv, mod[l], jnp.repeat(ssd_d[l], SSD_HEAD_DIM),
                        ssd_norm_g[l], w_out[l].astype(BF16), final_norm_g, n_ctx, final)
    return xall
```

```python
import functools
import math

import numpy as np
import jax
import jax.numpy as jnp
from jax import lax
from jax.experimental import pallas as pl
from jax.experimental.pallas import tpu as pltpu

F32 = jnp.float32
BF16 = jnp.bfloat16

EPS = 1e-6
GRID_W = 64
ROPE_BASE = 10000.0

SSD_HEADS = 12
SSD_HEAD_DIM = 64
SSD_WIDTH = SSD_HEADS * SSD_HEAD_DIM
SSD_GROUPS = 2
SSD_STATE = 128
SSD_BC = SSD_GROUPS * SSD_STATE
SSD_XBC = SSD_WIDTH + 2 * SSD_BC
SSD_CONV_LEN = 5
SSD_CHUNK = 128
ATTN_HEADS = 12
ATTN_KV_HEADS = 4
ATTN_REP = ATTN_HEADS // ATTN_KV_HEADS
HEAD_DIM = 64
ATTN_WIDTH = ATTN_HEADS * HEAD_DIM
KV_WIDTH = ATTN_KV_HEADS * HEAD_DIM
ROPE_NF = HEAD_DIM // 4
V_AUG = HEAD_DIM + 16
ATTN_UNROLL = 3
CM_WIDTH = 512
CM_CONV_LEN = 31
MIX_WIDTH = SSD_WIDTH + ATTN_WIDTH + CM_WIDTH

LANES = 128
SUBLANES = 8
DT_PAD = LANES
TILE = 256
SSD_HALO = 16
CM_HALO = 16
CTX_ROW = 4
MOD_ROWS = 8
VMEM_LIMIT = 56 * 1024 * 1024

PROJ_WIDTHS = (SSD_WIDTH, SSD_XBC, ATTN_WIDTH, KV_WIDTH, KV_WIDTH, ATTN_WIDTH,
               CM_WIDTH, CM_WIDTH, CM_WIDTH, DT_PAD)
PROJ_SPLIT = (2, 7, 1)


def _dot(a, b):
    return jnp.dot(a, b, preferred_element_type=F32)


def _silu(x):
    return x * jax.nn.sigmoid(x)


def _split3(x):
    hi = x.astype(BF16)
    r1 = x - hi.astype(F32)
    mid = r1.astype(BF16)
    lo = (r1 - mid.astype(F32)).astype(BF16)
    return hi, mid, lo


def _params(sem):
    return pltpu.CompilerParams(dimension_semantics=sem, vmem_limit_bytes=VMEM_LIMIT)


def _ada_kernel(c_ref, w_ref, b_ref, o_ref):
    s = _silu(c_ref[...])
    s_hi = s.astype(BF16)
    s_lo = (s - s_hi.astype(F32)).astype(BF16)
    w = w_ref[0]
    w_hi = w.astype(BF16)
    w_lo = (w - w_hi.astype(F32)).astype(BF16)
    o_ref[0] = _dot(s_hi, w_hi) + _dot(s_hi, w_lo) + _dot(s_lo, w_hi) + b_ref[0]


def _ada(cvec, ada_w, ada_b):
    depth, d, n3 = ada_w.shape
    tn = 1024
    return pl.pallas_call(
        _ada_kernel,
        out_shape=jax.ShapeDtypeStruct((depth, MOD_ROWS, n3), F32),
        grid=(depth, n3 // tn),
        in_specs=[
            pl.BlockSpec((MOD_ROWS, d), lambda l, j: (0, 0)),
            pl.BlockSpec((1, d, tn), lambda l, j: (l, 0, j)),
            pl.BlockSpec((1, 1, tn), lambda l, j: (l, 0, j)),
        ],
        out_specs=pl.BlockSpec((1, MOD_ROWS, tn), lambda l, j: (l, 0, j)),
        compiler_params=_params(("arbitrary", "arbitrary")),
        name="ada_mod",
    )(cvec, ada_w, ada_b.reshape(depth, 1, n3))


def _inproj_kernel(xc_ref, xl_ref, mod_ref, g_ref, wa_ref, wb_ref, wd_ref, *out_refs):
    d = xl_ref.shape[-1]
    b = pl.program_id(0)
    i = pl.program_id(1)
    row = jnp.where(i == 0, CTX_ROW, b)
    m = mod_ref[pl.ds(row, 1), :]
    shift = m[:, :d]
    scale = m[:, d:2 * d]
    x = jnp.where(i == 0, xc_ref[0], xl_ref[0])
    ms = jnp.mean(x * x, axis=-1, keepdims=True)
    h = (x * lax.rsqrt(ms + EPS) * g_ref[...]) * (1.0 + scale) + shift
    hb = h.astype(BF16)
    slabs = ((wa_ref, PROJ_SPLIT[0]), (wb_ref, PROJ_SPLIT[1]), (wd_ref, PROJ_SPLIT[2]))
    k = 0
    for w_ref, n_out in slabs:
        off = 0
        for o_ref, wd in zip(out_refs[k:k + n_out], PROJ_WIDTHS[k:k + n_out]):
            o_ref[0] = _dot(hb, w_ref[:, off:off + wd]).astype(o_ref.dtype)
            off += wd
        k += n_out


def _row_sources(ctx_src, lat_src, lat_off, d):
    return [pl.BlockSpec((1, TILE, d), lambda b, i: (b, 0, 0)),
            pl.BlockSpec((1, TILE, d), lambda b, i: (b, jnp.maximum(i - 1, 0) + lat_off, 0))]


def _inproj(ctx_src, lat_src, lat_off, mod_l, g, w_a, w_b, w_dt):
    nb, _, d = lat_src.shape
    nt = 1 + lat_src.shape[1] // TILE - lat_off
    t = nt * TILE
    dtypes = (BF16,) * (len(PROJ_WIDTHS) - 1) + (F32,)
    outs = tuple(jax.ShapeDtypeStruct((nb, t, wd), dt_) for wd, dt_ in zip(PROJ_WIDTHS, dtypes))
    return pl.pallas_call(
        _inproj_kernel,
        out_shape=outs,
        grid=(nb, nt),
        in_specs=_row_sources(ctx_src, lat_src, lat_off, d) + [
            pl.BlockSpec(mod_l.shape, lambda b, i: (0, 0)),
            pl.BlockSpec((1, d), lambda b, i: (0, 0)),
            pl.BlockSpec(w_a.shape, lambda b, i: (0, 0)),
            pl.BlockSpec(w_b.shape, lambda b, i: (0, 0)),
            pl.BlockSpec(w_dt.shape, lambda b, i: (0, 0)),
        ],
        out_specs=tuple(pl.BlockSpec((1, TILE, wd), lambda b, i: (b, i, 0)) for wd in PROJ_WIDTHS),
        compiler_params=_params(("arbitrary", "arbitrary")),
        name="in_proj",
    )(ctx_src, lat_src, mod_l, g.reshape(1, d), w_a, w_b, w_dt)


def _ssd_prep_kernel(xc_ref, xp_ref, xn_ref, dt_ref, cw_ref, cb_ref, dtb_ref, alog_ref, e3_ref,
                     xs_ref, xdtf_ref, xdtb_ref, cm_ref, bt_ref, cum_ref, cumt_ref,
                     ext_ref):
    i = pl.program_id(1)
    nt = pl.num_programs(1)
    prev_ok = i >= 2
    next_ok = jnp.logical_and(i != 0, i != nt - 1)
    ext_ref[0:SSD_HALO, :] = jnp.where(prev_ok, xp_ref[0].astype(F32), 0.0)
    ext_ref[SSD_HALO:SSD_HALO + TILE, :] = xc_ref[0].astype(F32)
    ext_ref[SSD_HALO + TILE:, :] = jnp.where(next_ok, xn_ref[0].astype(F32), 0.0)
    pad = SSD_CONV_LEN // 2
    acc = jnp.broadcast_to(cb_ref[...], (TILE, SSD_XBC))
    for k in range(SSD_CONV_LEN):
        acc = acc + ext_ref[pl.ds(SSD_HALO - pad + k, TILE), :] * cw_ref[k:k + 1, :]
    y = _silu(acc)
    xs = y[:, :SSD_WIDTH]
    bm = y[:, SSD_WIDTH:SSD_WIDTH + SSD_BC]
    xs_ref[0] = xs.astype(BF16)
    cm_ref[0] = y[:, SSD_WIDTH + SSD_BC:].astype(BF16)
    bt_ref[0] = bm.T.astype(BF16)

    lane = lax.broadcasted_iota(jnp.int32, (TILE, DT_PAD), 1)
    nh2 = 2 * SSD_HEADS
    z = dt_ref[0] + dtb_ref[...]
    dt = jnp.maximum(z, 0.0) + jnp.log1p(jnp.exp(-jnp.abs(z)))
    dt = jnp.where(lane < nh2, dt, 0.0)
    a = dt * (-jnp.exp(alog_ref[...]))
    a = jnp.where(lane < nh2, a, 0.0)

    d_hi, d_mid, d_lo = _split3(dt)
    d3 = (d_hi.astype(F32) + pltpu.roll(d_mid.astype(F32), nh2, axis=1)
          + pltpu.roll(d_lo.astype(F32), 2 * nh2, axis=1)).astype(BF16)
    dexp = _dot(d3, e3_ref[...])
    xdtf_ref[0] = (xs * dexp[:, :SSD_WIDTH]).astype(BF16)
    xdtb_ref[0] = (xs * dexp[:, SSD_WIDTH:]).astype(BF16)

    r = lax.broadcasted_iota(jnp.int32, (SSD_CHUNK, 3 * SSD_CHUNK), 0)
    c = lax.broadcasted_iota(jnp.int32, (SSD_CHUNK, 3 * SSD_CHUNK), 1) % SSD_CHUNK
    tri_l = jnp.where(c <= r, 1.0, 0.0).astype(BF16)
    tri_u = jnp.where(c >= r, 1.0, 0.0).astype(BF16)
    lane_c = lax.broadcasted_iota(jnp.int32, (SSD_CHUNK, DT_PAD), 1)
    cums = []
    for ch in range(TILE // SSD_CHUNK):
        a_c = a[ch * SSD_CHUNK:(ch + 1) * SSD_CHUNK, :]
        a3 = jnp.concatenate(_split3(a_c), axis=0)
        cums.append(jnp.where(lane_c < SSD_HEADS, _dot(tri_l, a3), _dot(tri_u, a3)))
    cum = jnp.concatenate(cums, axis=0)
    cum_ref[0] = cum
    cumt_ref[0] = cum.T[:cumt_ref.shape[1], :]


def _ssd_prep(xbc, dt_raw, conv_w, conv_b, dt_bias, a_log, e3):
    nb, t, _ = xbc.shape
    nt = t // TILE
    hb = TILE // SSD_HALO
    nhb = t // SSD_HALO
    cw = jnp.zeros((8, SSD_XBC), F32).at[:SSD_CONV_LEN].set(conv_w)
    pad1 = lambda v: jnp.zeros((1, DT_PAD), F32).at[0, :v.shape[0]].set(v)
    ctr = 32
    tok = lambda w, dt_: jax.ShapeDtypeStruct((nb, t, w), dt_)
    tile = lambda w: pl.BlockSpec((1, TILE, w), lambda b, i: (b, i, 0))
    const = lambda shp: pl.BlockSpec(shp, lambda b, i: tuple(0 for _ in shp))
    return pl.pallas_call(
        _ssd_prep_kernel,
        out_shape=(tok(SSD_WIDTH, BF16), tok(SSD_WIDTH, BF16), tok(SSD_WIDTH, BF16),
                   tok(SSD_BC, BF16),
                   jax.ShapeDtypeStruct((nb, SSD_BC, t), BF16),
                   tok(DT_PAD, F32),
                   jax.ShapeDtypeStruct((nb, ctr, t), F32)),
        grid=(nb, nt),
        in_specs=[
            tile(SSD_XBC),
            pl.BlockSpec((1, SSD_HALO, SSD_XBC), lambda b, i: (b, jnp.maximum(i * hb - 1, 0), 0)),
            pl.BlockSpec((1, SSD_HALO, SSD_XBC), lambda b, i: (b, jnp.minimum((i + 1) * hb, nhb - 1), 0)),
            tile(DT_PAD),
            const((8, SSD_XBC)), const((1, SSD_XBC)), const((1, DT_PAD)), const((1, DT_PAD)),
            const(e3.shape),
        ],
        out_specs=(tile(SSD_WIDTH), tile(SSD_WIDTH), tile(SSD_WIDTH), tile(SSD_BC),
                   pl.BlockSpec((1, SSD_BC, TILE), lambda b, i: (b, 0, i)),
                   tile(DT_PAD),
                   pl.BlockSpec((1, ctr, TILE), lambda b, i: (b, 0, i))),
        scratch_shapes=[pltpu.VMEM((TILE + 2 * SSD_HALO, SSD_XBC), F32)],
        compiler_params=_params(("arbitrary", "arbitrary")),
        name="ssd_prep",
    )(xbc, xbc, xbc, dt_raw, cw, conv_b.reshape(1, SSD_XBC), pad1(dt_bias), pad1(a_log), e3)


def _ssd_direction(xdt, cm, bt, cum, cumt, h_ref, col0, fwd):
    n = SSD_CHUNK
    li = lax.broadcasted_iota(jnp.int32, (n, n), 0)
    si = lax.broadcasted_iota(jnp.int32, (n, n), 1)
    mask = (li >= si) if fwd else (si >= li)
    lo_half = si < SSD_HEAD_DIM
    tot_row = n - 1 if fwd else 0
    hprev = h_ref[...]
    hb = hprev.astype(BF16)
    gw = SSD_WIDTH // SSD_GROUPS
    pairs = SSD_HEADS // SSD_GROUPS // 2
    y_slabs = []
    h_new = []
    for g in range(SSD_GROUPS):
        cg = cm[:, g * SSD_STATE:(g + 1) * SSD_STATE]
        btg = bt[g * SSD_STATE:(g + 1) * SSD_STATE, :]
        cb = _dot(cg, btg)
        yoff = _dot(cg, hb[:, g * gw:(g + 1) * gw])
        xw_slabs = []
        etot_slabs = []
        for jj in range(pairs):
            j = g * pairs + jj
            h0 = col0 + 2 * j
            cb0 = jnp.broadcast_to(cum[:, h0:h0 + 1], (n, n))
            cb1 = jnp.broadcast_to(cum[:, h0 + 1:h0 + 2], (n, n))
            l0 = jnp.exp(jnp.where(mask, cb0 - cumt[h0:h0 + 1, :], -jnp.inf))
            l1 = jnp.exp(jnp.where(mask, cb1 - cumt[h0 + 1:h0 + 2, :], -jnp.inf))
            mm = jnp.concatenate([(cb * l0).astype(BF16), (cb * l1).astype(BF16)], axis=1)
            slab = xdt[:, j * LANES:(j + 1) * LANES]
            zero = jnp.zeros_like(slab)
            ww = jnp.concatenate([jnp.where(lo_half, slab, zero), jnp.where(lo_half, zero, slab)], axis=0)
            ydiag = _dot(mm, ww)
            csel = jnp.where(lo_half, cb0, cb1)
            tot = csel[tot_row:tot_row + 1, :]
            y_slabs.append(ydiag + yoff[:, jj * LANES:(jj + 1) * LANES] * jnp.exp(csel))
            xw_slabs.append((slab.astype(F32) * jnp.exp(tot - csel)).astype(BF16))
            etot_slabs.append(jnp.exp(tot))
        xw = jnp.concatenate(xw_slabs, axis=1)
        etot = jnp.concatenate(etot_slabs, axis=1)
        h_new.append(hprev[:, g * gw:(g + 1) * gw] * etot + _dot(btg, xw))
    h_ref[...] = jnp.concatenate(h_new, axis=1)
    return jnp.concatenate(y_slabs, axis=1)


def _ssd_scan_kernel(xf_ref, cf_ref, btf_ref, cumf_ref, cumtf_ref,
                     xb_ref, cb_ref, btb_ref, cumb_ref, cumtb_ref,
                     yf_ref, yb_ref, hf_ref, hb_ref):
    @pl.when(pl.program_id(1) == 0)
    def _():
        hf_ref[...] = jnp.zeros_like(hf_ref)
        hb_ref[...] = jnp.zeros_like(hb_ref)

    n_ch = TILE // SSD_CHUNK
    for ch in range(n_ch):
        r = slice(ch * SSD_CHUNK, (ch + 1) * SSD_CHUNK)
        yf_ref[0, r, :] = _ssd_direction(xf_ref[0, r, :], cf_ref[0, r, :], btf_ref[0, :, r], cumf_ref[0, r, :],
                                         cumtf_ref[0, :, r], hf_ref, 0, True).astype(yf_ref.dtype)
    for ch in reversed(range(n_ch)):
        r = slice(ch * SSD_CHUNK, (ch + 1) * SSD_CHUNK)
        yb_ref[0, r, :] = _ssd_direction(xb_ref[0, r, :], cb_ref[0, r, :], btb_ref[0, :, r], cumb_ref[0, r, :],
                                         cumtb_ref[0, :, r], hb_ref, SSD_HEADS, False).astype(yb_ref.dtype)


def _ssd_scan(xdtf, xdtb, cm, bt, cum, cumt, n_ctx):
    nb, t, _ = xdtf.shape
    nt = t // TILE
    assert n_ctx == TILE
    ctr = cumt.shape[1]

    def bmap(s):
        return jnp.where(s == 0, 0, nt - s)

    fmap = lambda s: s
    rows = lambda w, tm: pl.BlockSpec((1, TILE, w), lambda b, s: (b, tm(s), 0))
    cols = lambda h, tm: pl.BlockSpec((1, h, TILE), lambda b, s: (b, 0, tm(s)))
    side = lambda tm: [rows(SSD_WIDTH, tm), rows(SSD_BC, tm), cols(SSD_BC, tm),
                       rows(DT_PAD, tm), cols(ctr, tm)]
    return pl.pallas_call(
        _ssd_scan_kernel,
        out_shape=(jax.ShapeDtypeStruct((nb, t, SSD_WIDTH), BF16),
                   jax.ShapeDtypeStruct((nb, t, SSD_WIDTH), BF16)),
        grid=(nb, nt),
        in_specs=side(fmap) + side(bmap),
        out_specs=(rows(SSD_WIDTH, fmap), rows(SSD_WIDTH, bmap)),
        scratch_shapes=[pltpu.VMEM((SSD_STATE, SSD_WIDTH), F32), pltpu.VMEM((SSD_STATE, SSD_WIDTH), F32)],
        compiler_params=_params(("arbitrary", "arbitrary")),
        name="ssd_scan",
    )(xdtf, cm, bt, cum, cumt, xdtb, cm, bt, cum, cumt)


def _head_norm(x, bd, g):
    sq = x * x
    hi = sq.astype(BF16)
    lo = (sq - hi.astype(F32)).astype(BF16)
    parts = []
    w = bd.shape[0]
    for s in range(x.shape[1] // w):
        sl = slice(s * w, (s + 1) * w)
        parts.append(_dot(hi[:, sl], bd) + _dot(lo[:, sl], bd))
    ss = jnp.concatenate(parts, axis=1) if len(parts) > 1 else parts[0]
    return x * lax.rsqrt(ss * (1.0 / HEAD_DIM) + EPS) * g


def _rope(x, cs, sn):
    w = x.shape[1]
    lane = lax.broadcasted_iota(jnp.int32, x.shape, 1)
    first_half = (lane % (2 * ROPE_NF)) < ROPE_NF
    partner = jnp.where(first_half, pltpu.roll(x, w - ROPE_NF, axis=1), pltpu.roll(x, ROPE_NF, axis=1))
    return x * cs + partner * sn


def _attn_prep_kernel(q_ref, k_ref, v_ref, cs_ref, sn_ref, bd_ref, qg_ref, kg_ref,
                      qt_ref, ko_ref, vt_ref):
    bd = bd_ref[...]
    cs = jnp.concatenate([cs_ref[...]] * (ATTN_WIDTH // LANES), axis=1)
    sn = jnp.concatenate([sn_ref[...]] * (ATTN_WIDTH // LANES), axis=1)
    q = _rope(_head_norm(q_ref[0].astype(F32), bd, qg_ref[...]), cs, sn)
    q = q * (HEAD_DIM ** -0.5 * math.log2(math.e))
    qt_ref[0] = q.T.astype(BF16)
    k = _rope(_head_norm(k_ref[0].astype(F32), bd, kg_ref[...]), cs[:, :KV_WIDTH], sn[:, :KV_WIDTH])
    ko_ref[0, 0] = k.astype(BF16)
    vt = v_ref[0].astype(F32).T.astype(BF16)
    ones_blk = jnp.where(lax.broadcasted_iota(jnp.int32, (V_AUG - HEAD_DIM, TILE), 0) == 0,
                         1.0, 0.0).astype(BF16)
    parts = []
    for g in range(ATTN_KV_HEADS):
        parts += [vt[g * HEAD_DIM:(g + 1) * HEAD_DIM, :], ones_blk]
    vt_ref[0, 0] = jnp.concatenate(parts, axis=0)


def _attn_prep(q, k, v, cs, sn, bd, qg, kg):
    nb, t, _ = q.shape
    nt = t // TILE
    tile = lambda w: pl.BlockSpec((1, TILE, w), lambda b, i: (b, i, 0))
    const = lambda shp: pl.BlockSpec(shp, lambda b, i: tuple(0 for _ in shp))
    return pl.pallas_call(
        _attn_prep_kernel,
        out_shape=(jax.ShapeDtypeStruct((nb, ATTN_WIDTH, t), BF16),
                   jax.ShapeDtypeStruct((nb, nt, TILE, KV_WIDTH), BF16),
                   jax.ShapeDtypeStruct((nb, nt, ATTN_KV_HEADS * V_AUG, TILE), BF16)),
        grid=(nb, nt),
        in_specs=[tile(ATTN_WIDTH), tile(KV_WIDTH), tile(KV_WIDTH),
                  pl.BlockSpec((TILE, LANES), lambda b, i: (i, 0)),
                  pl.BlockSpec((TILE, LANES), lambda b, i: (i, 0)),
                  const(bd.shape), const((1, ATTN_WIDTH)), const((1, KV_WIDTH))],
        out_specs=(pl.BlockSpec((1, ATTN_WIDTH, TILE), lambda b, i: (b, 0, i)),
                   pl.BlockSpec((1, 1, TILE, KV_WIDTH), lambda b, i: (b, i, 0, 0)),
                   pl.BlockSpec((1, 1, ATTN_KV_HEADS * V_AUG, TILE), lambda b, i: (b, i, 0, 0))),
        compiler_params=_params(("arbitrary", "arbitrary")),
        name="attn_prep",
    )(q, k, v, cs, sn, bd, qg, kg)


def _attn_kernel(qt_ref, k_ref, vt_ref, ga_ref, o_ref, qp_ref, s_ref, p_ref, acc_ref, ot_ref):
    i = pl.program_id(1)
    n_chunks = k_ref.shape[1]
    gq = ATTN_REP * HEAD_DIM
    wq = ATTN_REP * TILE
    groups = range(ATTN_KV_HEADS)
    zero = jnp.zeros((HEAD_DIM, TILE), BF16)
    for g in groups:
        upper = (g * HEAD_DIM) % LANES != 0
        for r in range(ATTN_REP):
            qh = qt_ref[0, g * gq + r * HEAD_DIM:g * gq + (r + 1) * HEAD_DIM, :]
            qp_ref[g, :, r * TILE:(r + 1) * TILE] = jnp.concatenate([zero, qh] if upper else [qh, zero], axis=0)
    acc_ref[...] = jnp.zeros_like(acc_ref)

    def qk_stage(g, c):
        lane_tile = (g * HEAD_DIM) // LANES
        kc = k_ref[0, c, :, lane_tile * LANES:(lane_tile + 1) * LANES]
        s = _dot(kc, qp_ref[g])
        s_ref[g] = s
        return jnp.max(s, axis=0, keepdims=True)

    def pv_stage(g, c, a_prev):
        vc = vt_ref[0, c, g * V_AUG:(g + 1) * V_AUG, :]
        acc_ref[g] = acc_ref[g] * a_prev + _dot(vc, p_ref[g])

    def step(c, carry, do_pv, do_qk):
        ms, a_prevs, cmaxs = carry
        new_m, new_a, new_cmax = [], [], []
        for g in groups:
            if do_pv:
                pv_stage(g, c - 1, a_prevs[g])
            m_new = jnp.maximum(ms[g], cmaxs[g])
            p_ref[g] = jnp.exp2(s_ref[g] - m_new).astype(BF16)
            new_cmax.append(qk_stage(g, c + 1) if do_qk else cmaxs[g])
            new_m.append(m_new)
            new_a.append(jnp.exp2(ms[g] - m_new))
        return tuple(new_m), tuple(new_a), tuple(new_cmax)

    def run(nk):
        carry = (tuple(jnp.full((1, wq), -jnp.inf, F32) for _ in groups),
                 tuple(jnp.ones((1, wq), F32) for _ in groups),
                 tuple(qk_stage(g, 0) for g in groups))
        carry = step(0, carry, False, nk > 1)
        if nk > 2:
            carry = lax.fori_loop(1, nk - 1, lambda c, cr: step(c, cr, True, True), carry,
                                  unroll=ATTN_UNROLL)
        if nk > 1:
            carry = step(nk - 1, carry, True, False)
        for g in groups:
            pv_stage(g, nk - 1, carry[1][g])
            acc = acc_ref[g]
            ot = acc[:HEAD_DIM, :] / acc[HEAD_DIM:HEAD_DIM + 1, :]
            for r in range(ATTN_REP):
                ot_ref[g * gq + r * HEAD_DIM:g * gq + (r + 1) * HEAD_DIM, :] = ot[:, r * TILE:(r + 1) * TILE]
        o_ref[0] = (ot_ref[...].T * _silu(ga_ref[0].astype(F32))).astype(o_ref.dtype)

    @pl.when(i == 0)
    def _():
        run(1)

    @pl.when(i != 0)
    def _():
        run(n_chunks)


def _attention(qt, kk, vt, ga):
    nb, t, _ = ga.shape
    nt = t // TILE
    return pl.pallas_call(
        _attn_kernel,
        out_shape=jax.ShapeDtypeStruct((nb, t, ATTN_WIDTH), BF16),
        grid=(nb, nt),
        in_specs=[pl.BlockSpec((1, ATTN_WIDTH, TILE), lambda b, i: (b, 0, i)),
                  pl.BlockSpec((1, nt, TILE, KV_WIDTH), lambda b, i: (b, 0, 0, 0)),
                  pl.BlockSpec((1, nt, ATTN_KV_HEADS * V_AUG, TILE), lambda b, i: (b, 0, 0, 0)),
                  pl.BlockSpec((1, TILE, ATTN_WIDTH), lambda b, i: (b, i, 0))],
        out_specs=pl.BlockSpec((1, TILE, ATTN_WIDTH), lambda b, i: (b, i, 0)),
        scratch_shapes=[pltpu.VMEM((ATTN_KV_HEADS, LANES, ATTN_REP * TILE), BF16),
                        pltpu.VMEM((ATTN_KV_HEADS, TILE, ATTN_REP * TILE), F32),
                        pltpu.VMEM((ATTN_KV_HEADS, TILE, ATTN_REP * TILE), BF16),
                        pltpu.VMEM((ATTN_KV_HEADS, V_AUG, ATTN_REP * TILE), F32),
                        pltpu.VMEM((ATTN_WIDTH, TILE), F32)],
        compiler_params=_params(("arbitrary", "arbitrary")),
        name="attention",
    )(qt, kk, vt, ga)


def _glu(a, b):
    return a.astype(F32) * jax.nn.sigmoid(b.astype(F32))


def _conformer_kernel(ua_ref, ub_ref, uap_ref, ubp_ref, uan_ref, ubn_ref, gc_ref,
                      cw_ref, cb_ref, lg_ref, lb_ref, pw_ref, pb_ref, o_ref, ext_ref, sh_ref):
    i = pl.program_id(1)
    nt = pl.num_programs(1)
    prev_ok = i >= 2
    next_ok = jnp.logical_and(i != 0, i != nt - 1)
    ext_ref[0:CM_HALO, :] = jnp.where(prev_ok, _glu(uap_ref[0], ubp_ref[0]), 0.0)
    ext_ref[CM_HALO:CM_HALO + TILE, :] = _glu(ua_ref[0], ub_ref[0])
    ext_ref[CM_HALO + TILE:, :] = jnp.where(next_ok, _glu(uan_ref[0], ubn_ref[0]), 0.0)
    first = CM_HALO - CM_CONV_LEN // 2
    for ph in range(SUBLANES):
        sh_ref[ph] = ext_ref[pl.ds(first + ph, sh_ref.shape[1]), :]
    acc = jnp.broadcast_to(cb_ref[...], (TILE, CM_WIDTH))
    for k in range(CM_CONV_LEN):
        base = k // SUBLANES * SUBLANES
        acc = acc + sh_ref[k % SUBLANES, base:base + TILE, :] * cw_ref[k:k + 1, :]
    mu = jnp.mean(acc, axis=-1, keepdims=True)
    xc = acc - mu
    var = jnp.mean(xc * xc, axis=-1, keepdims=True)
    u = _silu(xc * lax.rsqrt(var + EPS) * lg_ref[...] + lb_ref[...])
    u = _dot(u.astype(BF16), pw_ref[...]) + pb_ref[...]
    o_ref[0] = (u * _silu(gc_ref[0].astype(F32))).astype(o_ref.dtype)


def _conformer(ua, ub, gc, conv_w, conv_b, ln_g, ln_b, pw_w, pw_b):
    nb, t, _ = ua.shape
    nt = t // TILE
    hb = TILE // CM_HALO
    nhb = t // CM_HALO
    cw = jnp.zeros((32, CM_WIDTH), F32).at[:CM_CONV_LEN].set(conv_w)
    tile = pl.BlockSpec((1, TILE, CM_WIDTH), lambda b, i: (b, i, 0))
    prev = pl.BlockSpec((1, CM_HALO, CM_WIDTH), lambda b, i: (b, jnp.maximum(i * hb - 1, 0), 0))
    nxt = pl.BlockSpec((1, CM_HALO, CM_WIDTH), lambda b, i: (b, jnp.minimum((i + 1) * hb, nhb - 1), 0))
    const = lambda shp: pl.BlockSpec(shp, lambda b, i: tuple(0 for _ in shp))
    row = lambda v: v.reshape(1, CM_WIDTH)
    return pl.pallas_call(
        _conformer_kernel,
        out_shape=jax.ShapeDtypeStruct((nb, t, CM_WIDTH), BF16),
        grid=(nb, nt),
        in_specs=[tile, tile, prev, prev, nxt, nxt, tile,
                  const((32, CM_WIDTH)), const((1, CM_WIDTH)), const((1, CM_WIDTH)), const((1, CM_WIDTH)),
                  const((CM_WIDTH, CM_WIDTH)), const((1, CM_WIDTH))],
        out_specs=tile,
        scratch_shapes=[pltpu.VMEM((TILE + 2 * CM_HALO, CM_WIDTH), F32),
                        pltpu.VMEM((SUBLANES, TILE + (CM_CONV_LEN - 1) // SUBLANES * SUBLANES, CM_WIDTH), F32)],
        compiler_params=_params(("arbitrary", "arbitrary")),
        name="conformer",
    )(ua, ub, ua, ub, ua, ub, gc, cw, row(conv_b), row(ln_g), row(ln_b), pw_w.astype(BF16), row(pw_b))


def _outproj_kernel(xc_ref, xl_ref, yf_ref, yb_ref, xs_ref, z_ref, at_ref, cv_ref, mod_ref,
                    dsk_ref, ng_ref, w_ref, fg_ref, o_ref, *, tile_off, final):
    d = xl_ref.shape[-1]
    b = pl.program_id(0)
    i = pl.program_id(1) + tile_off
    row = jnp.where(i == 0, CTX_ROW, b)
    x = xl_ref[0] if final else jnp.where(i == 0, xc_ref[0], xl_ref[0])
    gate = mod_ref[pl.ds(row, 1), :][:, 2 * d:]
    y = yf_ref[0].astype(F32) + yb_ref[0].astype(F32) + dsk_ref[...] * xs_ref[0].astype(F32)
    y = y * _silu(z_ref[0].astype(F32))
    y = y * lax.rsqrt(jnp.mean(y * y, axis=-1, keepdims=True) + EPS) * ng_ref[...]
    cat = jnp.concatenate([y.astype(BF16), at_ref[0], cv_ref[0]], axis=1)
    xn = x + gate * _dot(cat, w_ref[...])
    if final:
        xn = xn * lax.rsqrt(jnp.mean(xn * xn, axis=-1, keepdims=True) + EPS) * fg_ref[...]
    o_ref[0] = xn


def _outproj(ctx_src, lat_src, lat_off, yf, yb, xs, z, at, cv, mod_l, d_skip_exp, norm_g, w_out,
             final_g, final):
    nb, t, _ = yf.shape
    d = lat_src.shape[-1]
    tile_off = 1 if final else 0
    nt = t // TILE - tile_off
    if final:
        x_specs = [pl.BlockSpec((1, TILE, d), lambda b, i: (b, 0, 0)),
                   pl.BlockSpec((1, TILE, d), lambda b, i: (b, i + lat_off, 0))]
    else:
        x_specs = _row_sources(ctx_src, lat_src, lat_off, d)
    tile = lambda w: pl.BlockSpec((1, TILE, w), lambda b, i: (b, i + tile_off, 0))
    const = lambda shp: pl.BlockSpec(shp, lambda b, i: tuple(0 for _ in shp))
    return pl.pallas_call(
        functools.partial(_outproj_kernel, tile_off=tile_off, final=final),
        out_shape=jax.ShapeDtypeStruct((nb, nt * TILE, d), F32),
        grid=(nb, nt),
        in_specs=x_specs + [tile(SSD_WIDTH), tile(SSD_WIDTH), tile(SSD_WIDTH), tile(SSD_WIDTH),
                  tile(ATTN_WIDTH), tile(CM_WIDTH), const(mod_l.shape),
                  const((1, SSD_WIDTH)), const((1, SSD_WIDTH)), const(w_out.shape), const((1, d))],
        out_specs=pl.BlockSpec((1, TILE, d), lambda b, i: (b, i, 0)),
        compiler_params=_params(("arbitrary", "arbitrary")),
        name="out_proj_final" if final else "out_proj",
    )(ctx_src, lat_src, yf, yb, xs, z, at, cv, mod_l, d_skip_exp.reshape(1, SSD_WIDTH),
      norm_g.reshape(1, SSD_WIDTH), w_out, final_g.reshape(1, d))


def _expand_matrix():
    nh2 = 2 * SSD_HEADS
    e = np.zeros((DT_PAD, 2 * SSD_WIDTH), np.float32)
    for src in range(nh2):
        for term in range(3):
            e[term * nh2 + src, src * SSD_HEAD_DIM:(src + 1) * SSD_HEAD_DIM] = 1.0
    return jnp.asarray(e, BF16)


def _block_ones(width):
    idx = np.arange(width) // HEAD_DIM
    return jnp.asarray((idx[:, None] == idx[None, :]).astype(np.float32), BF16)


def _rope_tables(seq, n_ctx):
    n_rows = seq // GRID_W
    rows = jnp.repeat(jnp.arange(n_rows, dtype=F32), GRID_W)
    cols = jnp.tile(jnp.arange(GRID_W, dtype=F32), n_rows)
    inv = ROPE_BASE ** (-jnp.arange(ROPE_NF, dtype=F32) / ROPE_NF)
    ar, ac = rows[:, None] * inv, cols[:, None] * inv
    cs = jnp.concatenate([jnp.cos(ar), jnp.cos(ar), jnp.cos(ac), jnp.cos(ac)], axis=1)
    sn = jnp.concatenate([-jnp.sin(ar), jnp.sin(ar), -jnp.sin(ac), jnp.sin(ac)], axis=1)
    cs = jnp.concatenate([jnp.ones((n_ctx, HEAD_DIM), F32), cs], axis=0)
    sn = jnp.concatenate([jnp.zeros((n_ctx, HEAD_DIM), F32), sn], axis=0)
    return jnp.tile(cs, (1, LANES // HEAD_DIM)), jnp.tile(sn, (1, LANES // HEAD_DIM))


def kernel(x, c, ctx, c_ctx, norm_g, ada_w, ada_b, w_in, ssd_conv_w, ssd_conv_b, ssd_dt_bias, ssd_a_log, ssd_d, ssd_norm_g, q_norm_g, k_norm_g, cm_conv_w, cm_conv_b, cm_ln_g, cm_ln_b, cm_pw_w, cm_pw_b, w_out, final_norm_g):
    nb, seq, d = x.shape
    n_ctx = ctx.shape[1]
    depth = w_in.shape[0]
    assert n_ctx == TILE and seq % TILE == 0 and seq % GRID_W == 0 and nb <= CTX_ROW
    nh2 = 2 * SSD_HEADS
    z_end = SSD_WIDTH + SSD_XBC

    cvec = jnp.zeros((MOD_ROWS, d), F32).at[:nb].set(c).at[CTX_ROW].set(c_ctx)
    mod = _ada(cvec, ada_w, ada_b)
    cs, sn = _rope_tables(seq, n_ctx)
    e3 = _expand_matrix()
    bd = _block_ones(2 * LANES)

    src = (ctx, x, 0)
    for l in range(depth):
        final = l == depth - 1
        w = w_in[l]
        w_a = w[:, :z_end].astype(BF16)
        w_b = w[:, z_end + nh2:].astype(BF16)
        w_dt = jnp.pad(w[:, z_end:z_end + nh2], ((0, 0), (0, DT_PAD - nh2))).astype(BF16)
        z, xbc, q, k, v, ga, ua, ub, gc, dtr = _inproj(*src, mod[l], norm_g[l], w_a, w_b, w_dt)
        xs, xdtf, xdtb, cm, bt, cum, cumt = _ssd_prep(
            xbc, dtr, ssd_conv_w[l], ssd_conv_b[l], ssd_dt_bias[l], ssd_a_log[l], e3)
        yf, yb = _ssd_scan(xdtf, xdtb, cm, bt, cum, cumt, n_ctx)
        qt, kk, vt = _attn_prep(q, k, v, cs, sn, bd,
                                jnp.tile(q_norm_g[l], ATTN_HEADS).reshape(1, ATTN_WIDTH),
                                jnp.tile(k_norm_g[l], ATTN_KV_HEADS).reshape(1, KV_WIDTH))
        at = _attention(qt, kk, vt, ga)
        cv = _conformer(ua, ub, gc, cm_conv_w[l], cm_conv_b[l], cm_ln_g[l], cm_ln_b[l],
                        cm_pw_w[l], cm_pw_b[l])
        out = _outproj(*src, yf, yb, xs, z, at, cv, mod[l], jnp.repeat(ssd_d[l], SSD_HEAD_DIM),
                       ssd_norm_g[l], w_out[l].astype(BF16), final_norm_g, final)
        src = (out, out, 1)
    return out
```

```python
import functools
import math

import numpy as np
import jax
import jax.numpy as jnp
from jax import lax
from jax.experimental import pallas as pl
from jax.experimental.pallas import tpu as pltpu

F32 = jnp.float32
BF16 = jnp.bfloat16

EPS = 1e-6
GRID_W = 64
ROPE_BASE = 10000.0

SSD_HEADS = 12
SSD_HEAD_DIM = 64
SSD_WIDTH = SSD_HEADS * SSD_HEAD_DIM
SSD_GROUPS = 2
SSD_STATE = 128
SSD_BC = SSD_GROUPS * SSD_STATE
SSD_XBC = SSD_WIDTH + 2 * SSD_BC
SSD_CONV_LEN = 5
SSD_CHUNK = 128
ATTN_HEADS = 12
ATTN_KV_HEADS = 4
ATTN_REP = ATTN_HEADS // ATTN_KV_HEADS
HEAD_DIM = 64
ATTN_WIDTH = ATTN_HEADS * HEAD_DIM
KV_WIDTH = ATTN_KV_HEADS * HEAD_DIM
ROPE_NF = HEAD_DIM // 4
V_AUG = HEAD_DIM + 16
ATTN_UNROLL = 5
CM_WIDTH = 512
CM_CONV_LEN = 31

LANES = 128
SUBLANES = 8
DT_PAD = LANES
TILE = 256
HALO = 16
CTX_ROW = 4
MOD_ROWS = 8
VMEM_LIMIT = 56 * 1024 * 1024

PROJ_OFFSETS = dict(z=0, xbc=SSD_WIDTH,
                    q=0, k=ATTN_WIDTH, v=ATTN_WIDTH + KV_WIDTH, ga=ATTN_WIDTH + 2 * KV_WIDTH,
                    ua=2 * ATTN_WIDTH + 2 * KV_WIDTH, ub=2 * ATTN_WIDTH + 2 * KV_WIDTH + CM_WIDTH,
                    gc=2 * ATTN_WIDTH + 2 * KV_WIDTH + 2 * CM_WIDTH)


def _dot(a, b):
    return jnp.dot(a, b, preferred_element_type=F32)


def _silu(x):
    return x * jax.nn.sigmoid(x)


def _split3(x):
    hi = x.astype(BF16)
    r1 = x - hi.astype(F32)
    mid = r1.astype(BF16)
    lo = (r1 - mid.astype(F32)).astype(BF16)
    return hi, mid, lo


def _params(sem):
    return pltpu.CompilerParams(dimension_semantics=sem, vmem_limit_bytes=VMEM_LIMIT)


def _ada_kernel(c_ref, w_ref, b_ref, o_ref):
    s = _silu(c_ref[...])
    s_hi = s.astype(BF16)
    s_lo = (s - s_hi.astype(F32)).astype(BF16)
    w = w_ref[0]
    w_hi = w.astype(BF16)
    w_lo = (w - w_hi.astype(F32)).astype(BF16)
    o_ref[0] = _dot(s_hi, w_hi) + _dot(s_hi, w_lo) + _dot(s_lo, w_hi) + b_ref[0]


def _ada(cvec, ada_w, ada_b):
    depth, d, n3 = ada_w.shape
    tn = 1024
    return pl.pallas_call(
        _ada_kernel,
        out_shape=jax.ShapeDtypeStruct((depth, MOD_ROWS, n3), F32),
        grid=(depth, n3 // tn),
        in_specs=[
            pl.BlockSpec((MOD_ROWS, d), lambda l, j: (0, 0)),
            pl.BlockSpec((1, d, tn), lambda l, j: (l, 0, j)),
            pl.BlockSpec((1, 1, tn), lambda l, j: (l, 0, j)),
        ],
        out_specs=pl.BlockSpec((1, MOD_ROWS, tn), lambda l, j: (l, 0, j)),
        compiler_params=_params(("arbitrary", "arbitrary")),
        name="ada_mod",
    )(cvec, ada_w, ada_b.reshape(depth, 1, n3))


def _row_sources(ctx_src, lat_src, lat_off, d):
    return [pl.BlockSpec((1, TILE, d), lambda b, i: (b, 0, 0)),
            pl.BlockSpec((1, TILE, d), lambda b, i: (b, jnp.maximum(i - 1, 0) + lat_off, 0))]


def _head_norm(x, bd, g):
    sq = x * x
    hi = sq.astype(BF16)
    lo = (sq - hi.astype(F32)).astype(BF16)
    parts = []
    w = bd.shape[0]
    for s in range(x.shape[1] // w):
        sl = slice(s * w, (s + 1) * w)
        parts.append(_dot(hi[:, sl], bd) + _dot(lo[:, sl], bd))
    ss = jnp.concatenate(parts, axis=1) if len(parts) > 1 else parts[0]
    return x * lax.rsqrt(ss * (1.0 / HEAD_DIM) + EPS) * g


def _rope(x, cs, sn):
    w = x.shape[1]
    lane = lax.broadcasted_iota(jnp.int32, x.shape, 1)
    first_half = (lane % (2 * ROPE_NF)) < ROPE_NF
    partner = jnp.where(first_half, pltpu.roll(x, w - ROPE_NF, axis=1), pltpu.roll(x, ROPE_NF, axis=1))
    return x * cs + partner * sn


def _ssd_prep(sext_ref, dt_raw, cw_ref, cb_ref, dtb_ref, alog_ref, e3_ref,
              xs_ref, xdtf_ref, xdtb_ref, cm_ref, bt_ref, cum_ref, cumt_ref):
    acc = jnp.broadcast_to(cb_ref[...], (TILE, SSD_XBC))
    xe = sext_ref[...]
    rows = xe.shape[0]
    for k in range(SSD_CONV_LEN):
        off = HALO - SSD_CONV_LEN // 2 + k
        acc = acc + pltpu.roll(xe, rows - off, axis=0)[:TILE, :] * cw_ref[k:k + 1, :]
    y = _silu(acc)
    xs = y[:, :SSD_WIDTH]
    bm = y[:, SSD_WIDTH:SSD_WIDTH + SSD_BC]
    xs_ref[0] = xs.astype(BF16)
    cm_ref[0] = y[:, SSD_WIDTH + SSD_BC:].astype(BF16)
    bt_ref[0] = bm.T.astype(BF16)

    lane = lax.broadcasted_iota(jnp.int32, (TILE, DT_PAD), 1)
    nh2 = 2 * SSD_HEADS
    z = dt_raw + dtb_ref[...]
    dt = jnp.maximum(z, 0.0) + jnp.log1p(jnp.exp(-jnp.abs(z)))
    dt = jnp.where(lane < nh2, dt, 0.0)
    a = dt * (-jnp.exp(alog_ref[...]))
    a = jnp.where(lane < nh2, a, 0.0)

    d_hi, d_mid, d_lo = _split3(dt)
    d3 = (d_hi.astype(F32) + pltpu.roll(d_mid.astype(F32), nh2, axis=1)
          + pltpu.roll(d_lo.astype(F32), 2 * nh2, axis=1)).astype(BF16)
    dexp = _dot(d3, e3_ref[...])
    xdtf_ref[0] = (xs * dexp[:, :SSD_WIDTH]).astype(BF16)
    xdtb_ref[0] = (xs * dexp[:, SSD_WIDTH:]).astype(BF16)

    r = lax.broadcasted_iota(jnp.int32, (SSD_CHUNK, 3 * SSD_CHUNK), 0)
    c = lax.broadcasted_iota(jnp.int32, (SSD_CHUNK, 3 * SSD_CHUNK), 1) % SSD_CHUNK
    tri_l = jnp.where(c <= r, 1.0, 0.0).astype(BF16)
    tri_u = jnp.where(c >= r, 1.0, 0.0).astype(BF16)
    lane_c = lax.broadcasted_iota(jnp.int32, (SSD_CHUNK, DT_PAD), 1)
    cums = []
    for ch in range(TILE // SSD_CHUNK):
        a_c = a[ch * SSD_CHUNK:(ch + 1) * SSD_CHUNK, :]
        a3 = jnp.concatenate(_split3(a_c), axis=0)
        cums.append(jnp.where(lane_c < SSD_HEADS, _dot(tri_l, a3), _dot(tri_u, a3)))
    cum = jnp.concatenate(cums, axis=0) * math.log2(math.e)
    cum_ref[0] = cum
    cumt_ref[0] = cum.T[:cumt_ref.shape[1], :]


def _attn_prep(q, k, v, cs_ref, sn_ref, bd_ref, qg_ref, kg_ref, qt_ref, ko_ref, vt_ref):
    bd = bd_ref[...]
    cs = jnp.concatenate([cs_ref[...]] * (ATTN_WIDTH // LANES), axis=1)
    sn = jnp.concatenate([sn_ref[...]] * (ATTN_WIDTH // LANES), axis=1)
    q = _rope(_head_norm(q, bd, qg_ref[...]), cs, sn)
    q = q * (HEAD_DIM ** -0.5 * math.log2(math.e))
    qt_ref[0] = q.T.astype(BF16)
    k = _rope(_head_norm(k, bd, kg_ref[...]), cs[:, :KV_WIDTH], sn[:, :KV_WIDTH])
    ko_ref[0, 0] = k.astype(BF16)
    vt = v.T.astype(BF16)
    ones_blk = jnp.where(lax.broadcasted_iota(jnp.int32, (V_AUG - HEAD_DIM, TILE), 0) == 0,
                         1.0, 0.0).astype(BF16)
    parts = []
    for g in range(ATTN_KV_HEADS):
        parts += [vt[g * HEAD_DIM:(g + 1) * HEAD_DIM, :], ones_blk]
    vt_ref[0, 0] = jnp.concatenate(parts, axis=0)


def _conformer(cext_ref, sh_ref, gc, cw_ref, cb_ref, lg_ref, lb_ref, pw_ref, pb_ref, o_ref):
    first = HALO - CM_CONV_LEN // 2
    ce = cext_ref[...]
    for ph in range(SUBLANES):
        sh_ref[ph] = pltpu.roll(ce, ce.shape[0] - (first + ph), axis=0)[:sh_ref.shape[1], :]
    acc = jnp.broadcast_to(cb_ref[...], (TILE, CM_WIDTH))
    for k in range(CM_CONV_LEN):
        base = k // SUBLANES * SUBLANES
        acc = acc + sh_ref[k % SUBLANES, base:base + TILE, :] * cw_ref[k:k + 1, :]
    mu = jnp.mean(acc, axis=-1, keepdims=True)
    xc = acc - mu
    var = jnp.mean(xc * xc, axis=-1, keepdims=True)
    u = _silu(xc * lax.rsqrt(var + EPS) * lg_ref[...] + lb_ref[...])
    u = _dot(u.astype(BF16), pw_ref[...]) + pb_ref[...]
    o_ref[0] = (u * _silu(gc)).astype(o_ref.dtype)


def _front_kernel(xc_ref, xl_ref, xp_ref, xn_ref, mod_ref, g_ref, wa_ref, wb_ref, wd_ref,
                  scw_ref, scb_ref, dtb_ref, alog_ref, e3_ref,
                  cs_ref, sn_ref, bd_ref, qg_ref, kg_ref,
                  ccw_ref, ccb_ref, lg_ref, lb_ref, pw_ref, pb_ref,
                  z_ref, ga_ref, xs_ref, xdtf_ref, xdtb_ref, cm_ref, bt_ref, cum_ref, cumt_ref,
                  qt_ref, ko_ref, vt_ref, cv_ref,
                  sext_ref, cext_ref, sh_ref):
    d = xl_ref.shape[-1]
    b = pl.program_id(0)
    i = pl.program_id(1)
    nt = pl.num_programs(1)
    row = jnp.where(i == 0, CTX_ROW, b)
    m = mod_ref[pl.ds(row, 1), :]
    shift = m[:, :d]
    scale1 = 1.0 + m[:, d:2 * d]
    prev_ok = i >= 2
    next_ok = jnp.logical_and(i != 0, i != nt - 1)

    def mod_norm(x):
        ms = jnp.mean(x * x, axis=-1, keepdims=True)
        return ((x * lax.rsqrt(ms + EPS) * g_ref[...]) * scale1 + shift).astype(BF16)

    x = jnp.where(i == 0, xc_ref[0], xl_ref[0])
    hb = jnp.concatenate([mod_norm(xp_ref[0]), mod_norm(x), mod_norm(xn_ref[0])], axis=0)
    hm = hb[HALO:HALO + TILE]

    def halo_masked(v):
        r = lax.broadcasted_iota(jnp.int32, v.shape, 0)
        ok = jnp.logical_and(jnp.logical_or(r >= HALO, prev_ok), jnp.logical_or(r < HALO + TILE, next_ok))
        return jnp.where(ok, v, 0.0)

    o = PROJ_OFFSETS
    ua = _dot(hb, wb_ref[:, o["ua"]:o["ua"] + CM_WIDTH])
    ub = _dot(hb, wb_ref[:, o["ub"]:o["ub"] + CM_WIDTH])
    cext_ref[...] = halo_masked(ua * jax.nn.sigmoid(ub))
    sext_ref[...] = halo_masked(_dot(hb, wa_ref[:, o["xbc"]:o["xbc"] + SSD_XBC]))
    dt_raw = _dot(hm, wd_ref[...])
    q = _dot(hm, wb_ref[:, o["q"]:o["q"] + ATTN_WIDTH])
    k = _dot(hm, wb_ref[:, o["k"]:o["k"] + KV_WIDTH])
    v = _dot(hm, wb_ref[:, o["v"]:o["v"] + KV_WIDTH])
    gc = _dot(hm, wb_ref[:, o["gc"]:o["gc"] + CM_WIDTH])
    z_ref[0] = _dot(hm, wa_ref[:, o["z"]:o["z"] + SSD_WIDTH]).astype(BF16)
    ga_ref[0] = _dot(hm, wb_ref[:, o["ga"]:o["ga"] + ATTN_WIDTH]).astype(BF16)
    _attn_prep(q, k, v, cs_ref, sn_ref, bd_ref, qg_ref, kg_ref, qt_ref, ko_ref, vt_ref)
    _ssd_prep(sext_ref, dt_raw, scw_ref, scb_ref, dtb_ref, alog_ref, e3_ref,
              xs_ref, xdtf_ref, xdtb_ref, cm_ref, bt_ref, cum_ref, cumt_ref)
    _conformer(cext_ref, sh_ref, gc, ccw_ref, ccb_ref, lg_ref, lb_ref, pw_ref, pb_ref, cv_ref)


def _front(ctx_src, lat_src, lat_off, mod_l, g, w_a, w_b, w_dt, ssd_p, attn_p, cm_p):
    nb, lat_rows, d = lat_src.shape
    nt = 1 + lat_rows // TILE - lat_off
    t = nt * TILE
    hpt = TILE // HALO
    n_hb = lat_rows // HALO

    def lat_tile(i):
        return jnp.maximum(i - 1, 0) + lat_off

    tok = lambda w, dt_: jax.ShapeDtypeStruct((nb, t, w), dt_)
    tile = lambda w: pl.BlockSpec((1, TILE, w), lambda b, i: (b, i, 0))
    const = lambda a: pl.BlockSpec(a.shape, lambda b, i: tuple(0 for _ in a.shape))
    consts = (mod_l, g.reshape(1, d), w_a, w_b, w_dt) + ssd_p
    attn_consts = attn_p[2:]
    ctr = 32
    return pl.pallas_call(
        _front_kernel,
        out_shape=(tok(SSD_WIDTH, BF16), tok(ATTN_WIDTH, BF16),
                   tok(SSD_WIDTH, BF16), tok(SSD_WIDTH, BF16), tok(SSD_WIDTH, BF16), tok(SSD_BC, BF16),
                   jax.ShapeDtypeStruct((nb, SSD_BC, t), BF16), tok(DT_PAD, F32),
                   jax.ShapeDtypeStruct((nb, ctr, t), F32),
                   jax.ShapeDtypeStruct((nb, ATTN_WIDTH, t), BF16),
                   jax.ShapeDtypeStruct((nb, nt, TILE, KV_WIDTH), BF16),
                   jax.ShapeDtypeStruct((nb, nt, ATTN_KV_HEADS * V_AUG, TILE), BF16),
                   tok(CM_WIDTH, BF16)),
        grid=(nb, nt),
        in_specs=_row_sources(ctx_src, lat_src, lat_off, d) + [
            pl.BlockSpec((1, HALO, d), lambda b, i: (b, jnp.maximum(lat_tile(i) * hpt - 1, 0), 0)),
            pl.BlockSpec((1, HALO, d), lambda b, i: (b, jnp.minimum((lat_tile(i) + 1) * hpt, n_hb - 1), 0)),
        ] + [const(a) for a in consts] + [
            pl.BlockSpec((TILE, LANES), lambda b, i: (i, 0)),
            pl.BlockSpec((TILE, LANES), lambda b, i: (i, 0)),
        ] + [const(a) for a in attn_consts] + [const(a) for a in cm_p],
        out_specs=(tile(SSD_WIDTH), tile(ATTN_WIDTH),
                   tile(SSD_WIDTH), tile(SSD_WIDTH), tile(SSD_WIDTH), tile(SSD_BC),
                   pl.BlockSpec((1, SSD_BC, TILE), lambda b, i: (b, 0, i)), tile(DT_PAD),
                   pl.BlockSpec((1, ctr, TILE), lambda b, i: (b, 0, i)),
                   pl.BlockSpec((1, ATTN_WIDTH, TILE), lambda b, i: (b, 0, i)),
                   pl.BlockSpec((1, 1, TILE, KV_WIDTH), lambda b, i: (b, i, 0, 0)),
                   pl.BlockSpec((1, 1, ATTN_KV_HEADS * V_AUG, TILE), lambda b, i: (b, i, 0, 0)),
                   tile(CM_WIDTH)),
        scratch_shapes=[pltpu.VMEM((TILE + 2 * HALO, SSD_XBC), F32),
                        pltpu.VMEM((TILE + 2 * HALO, CM_WIDTH), F32),
                        pltpu.VMEM((SUBLANES, TILE + (CM_CONV_LEN - 1) // SUBLANES * SUBLANES, CM_WIDTH), F32)],
        compiler_params=_params(("arbitrary", "arbitrary")),
        name="front",
    )(ctx_src, lat_src, lat_src, lat_src, *consts, attn_p[0], attn_p[1], *attn_consts, *cm_p)


def _ssd_direction(xdt, cm, bt, cum, cumt, h_ref, col0, fwd):
    n = SSD_CHUNK
    li = lax.broadcasted_iota(jnp.int32, (n, n), 0)
    si = lax.broadcasted_iota(jnp.int32, (n, n), 1)
    mask = (li >= si) if fwd else (si >= li)
    lo_half = si < SSD_HEAD_DIM
    tot_row = n - 1 if fwd else 0
    hprev = h_ref[...]
    hb = hprev.astype(BF16)
    gw = SSD_WIDTH // SSD_GROUPS
    pairs = SSD_HEADS // SSD_GROUPS // 2
    y_slabs = []
    h_new = []
    for g in range(SSD_GROUPS):
        cg = cm[:, g * SSD_STATE:(g + 1) * SSD_STATE]
        btg = bt[g * SSD_STATE:(g + 1) * SSD_STATE, :]
        cb = _dot(cg, btg)
        yoff = _dot(cg, hb[:, g * gw:(g + 1) * gw])
        xw_slabs = []
        etot_slabs = []
        for jj in range(pairs):
            j = g * pairs + jj
            h0 = col0 + 2 * j
            cb0 = jnp.broadcast_to(cum[:, h0:h0 + 1], (n, n))
            cb1 = jnp.broadcast_to(cum[:, h0 + 1:h0 + 2], (n, n))
            l0 = jnp.exp2(jnp.where(mask, cb0 - cumt[h0:h0 + 1, :], -jnp.inf))
            l1 = jnp.exp2(jnp.where(mask, cb1 - cumt[h0 + 1:h0 + 2, :], -jnp.inf))
            mm = jnp.concatenate([(cb * l0).astype(BF16), (cb * l1).astype(BF16)], axis=1)
            slab = xdt[:, j * LANES:(j + 1) * LANES]
            zero = jnp.zeros_like(slab)
            ww = jnp.concatenate([jnp.where(lo_half, slab, zero), jnp.where(lo_half, zero, slab)], axis=0)
            ydiag = _dot(mm, ww)
            csel = jnp.where(lo_half, cb0, cb1)
            tot = csel[tot_row:tot_row + 1, :]
            y_slabs.append(ydiag + yoff[:, jj * LANES:(jj + 1) * LANES] * jnp.exp2(csel))
            xw_slabs.append((slab.astype(F32) * jnp.exp2(tot - csel)).astype(BF16))
            etot_slabs.append(jnp.exp2(tot))
        xw = jnp.concatenate(xw_slabs, axis=1)
        etot = jnp.concatenate(etot_slabs, axis=1)
        h_new.append(hprev[:, g * gw:(g + 1) * gw] * etot + _dot(btg, xw))
    h_ref[...] = jnp.concatenate(h_new, axis=1)
    return jnp.concatenate(y_slabs, axis=1)


def _ssd_scan_kernel(xf_ref, cf_ref, btf_ref, cumf_ref, cumtf_ref,
                     xb_ref, cb_ref, btb_ref, cumb_ref, cumtb_ref,
                     yf_ref, yb_ref, hf_ref, hb_ref):
    @pl.when(pl.program_id(1) == 0)
    def _():
        hf_ref[...] = jnp.zeros_like(hf_ref)
        hb_ref[...] = jnp.zeros_like(hb_ref)

    n_ch = TILE // SSD_CHUNK
    for ch in range(n_ch):
        r = slice(ch * SSD_CHUNK, (ch + 1) * SSD_CHUNK)
        yf_ref[0, r, :] = _ssd_direction(xf_ref[0, r, :], cf_ref[0, r, :], btf_ref[0, :, r], cumf_ref[0, r, :],
                                         cumtf_ref[0, :, r], hf_ref, 0, True).astype(yf_ref.dtype)
    for ch in reversed(range(n_ch)):
        r = slice(ch * SSD_CHUNK, (ch + 1) * SSD_CHUNK)
        yb_ref[0, r, :] = _ssd_direction(xb_ref[0, r, :], cb_ref[0, r, :], btb_ref[0, :, r], cumb_ref[0, r, :],
                                         cumtb_ref[0, :, r], hb_ref, SSD_HEADS, False).astype(yb_ref.dtype)


def _ssd_scan(xdtf, xdtb, cm, bt, cum, cumt, n_ctx):
    nb, t, _ = xdtf.shape
    nt = t // TILE
    assert n_ctx == TILE
    ctr = cumt.shape[1]

    def bmap(s):
        return jnp.where(s == 0, 0, nt - s)

    fmap = lambda s: s
    rows = lambda w, tm: pl.BlockSpec((1, TILE, w), lambda b, s: (b, tm(s), 0))
    cols = lambda h, tm: pl.BlockSpec((1, h, TILE), lambda b, s: (b, 0, tm(s)))
    side = lambda tm: [rows(SSD_WIDTH, tm), rows(SSD_BC, tm), cols(SSD_BC, tm),
                       rows(DT_PAD, tm), cols(ctr, tm)]
    return pl.pallas_call(
        _ssd_scan_kernel,
        out_shape=(jax.ShapeDtypeStruct((nb, t, SSD_WIDTH), BF16),
                   jax.ShapeDtypeStruct((nb, t, SSD_WIDTH), BF16)),
        grid=(nb, nt),
        in_specs=side(fmap) + side(bmap),
        out_specs=(rows(SSD_WIDTH, fmap), rows(SSD_WIDTH, bmap)),
        scratch_shapes=[pltpu.VMEM((SSD_STATE, SSD_WIDTH), F32), pltpu.VMEM((SSD_STATE, SSD_WIDTH), F32)],
        compiler_params=_params(("arbitrary", "arbitrary")),
        name="ssd_scan",
    )(xdtf, cm, bt, cum, cumt, xdtb, cm, bt, cum, cumt)


def _attn_kernel(qt_ref, k_ref, vt_ref, ga_ref, o_ref, qp_ref, s_ref, p_ref, acc_ref, ot_ref):
    i = pl.program_id(1)
    n_chunks = k_ref.shape[1]
    gq = ATTN_REP * HEAD_DIM
    wq = ATTN_REP * TILE
    groups = range(ATTN_KV_HEADS)
    zero = jnp.zeros((HEAD_DIM, TILE), BF16)
    for g in groups:
        upper = (g * HEAD_DIM) % LANES != 0
        for r in range(ATTN_REP):
            qh = qt_ref[0, g * gq + r * HEAD_DIM:g * gq + (r + 1) * HEAD_DIM, :]
            qp_ref[g, :, r * TILE:(r + 1) * TILE] = jnp.concatenate([zero, qh] if upper else [qh, zero], axis=0)
    acc_ref[...] = jnp.zeros_like(acc_ref)

    def qk_stage(g, c):
        lane_tile = (g * HEAD_DIM) // LANES
        kc = k_ref[0, c, :, lane_tile * LANES:(lane_tile + 1) * LANES]
        s = _dot(kc, qp_ref[g])
        s_ref[g] = s
        return jnp.max(s, axis=0, keepdims=True)

    def pv_stage(g, c, a_prev):
        vc = vt_ref[0, c, g * V_AUG:(g + 1) * V_AUG, :]
        acc_ref[g] = acc_ref[g] * a_prev + _dot(vc, p_ref[g])

    def step(c, carry, do_pv, do_qk):
        ms, a_prevs, cmaxs = carry
        new_m, new_a, new_cmax = [], [], []
        for g in groups:
            if do_pv:
                pv_stage(g, c - 1, a_prevs[g])
            m_new = jnp.maximum(ms[g], cmaxs[g])
            p_ref[g] = jnp.exp2(s_ref[g] - m_new).astype(BF16)
            new_cmax.append(qk_stage(g, c + 1) if do_qk else cmaxs[g])
            new_m.append(m_new)
            new_a.append(jnp.exp2(ms[g] - m_new))
        return tuple(new_m), tuple(new_a), tuple(new_cmax)

    def run(nk):
        carry = (tuple(jnp.full((1, wq), -jnp.inf, F32) for _ in groups),
                 tuple(jnp.ones((1, wq), F32) for _ in groups),
                 tuple(qk_stage(g, 0) for g in groups))
        carry = step(0, carry, False, nk > 1)
        if nk > 2:
            carry = lax.fori_loop(1, nk - 1, lambda c, cr: step(c, cr, True, True), carry,
                                  unroll=ATTN_UNROLL)
        if nk > 1:
            carry = step(nk - 1, carry, True, False)
        for g in groups:
            pv_stage(g, nk - 1, carry[1][g])
            acc = acc_ref[g]
            ot = acc[:HEAD_DIM, :] / acc[HEAD_DIM:HEAD_DIM + 1, :]
            for r in range(ATTN_REP):
                ot_ref[g * gq + r * HEAD_DIM:g * gq + (r + 1) * HEAD_DIM, :] = ot[:, r * TILE:(r + 1) * TILE]
        o_ref[0] = (ot_ref[...].T * _silu(ga_ref[0].astype(F32))).astype(o_ref.dtype)

    @pl.when(i == 0)
    def _():
        run(1)

    @pl.when(i != 0)
    def _():
        run(n_chunks)


def _attention(qt, kk, vt, ga):
    nb, t, _ = ga.shape
    nt = t // TILE
    return pl.pallas_call(
        _attn_kernel,
        out_shape=jax.ShapeDtypeStruct((nb, t, ATTN_WIDTH), BF16),
        grid=(nb, nt),
        in_specs=[pl.BlockSpec((1, ATTN_WIDTH, TILE), lambda b, i: (b, 0, i)),
                  pl.BlockSpec((1, nt, TILE, KV_WIDTH), lambda b, i: (b, 0, 0, 0)),
                  pl.BlockSpec((1, nt, ATTN_KV_HEADS * V_AUG, TILE), lambda b, i: (b, 0, 0, 0)),
                  pl.BlockSpec((1, TILE, ATTN_WIDTH), lambda b, i: (b, i, 0))],
        out_specs=pl.BlockSpec((1, TILE, ATTN_WIDTH), lambda b, i: (b, i, 0)),
        scratch_shapes=[pltpu.VMEM((ATTN_KV_HEADS, LANES, ATTN_REP * TILE), BF16),
                        pltpu.VMEM((ATTN_KV_HEADS, TILE, ATTN_REP * TILE), F32),
                        pltpu.VMEM((ATTN_KV_HEADS, TILE, ATTN_REP * TILE), BF16),
                        pltpu.VMEM((ATTN_KV_HEADS, V_AUG, ATTN_REP * TILE), F32),
                        pltpu.VMEM((ATTN_WIDTH, TILE), F32)],
        compiler_params=_params(("arbitrary", "arbitrary")),
        name="attention",
    )(qt, kk, vt, ga)


def _outproj_kernel(xc_ref, xl_ref, yf_ref, yb_ref, xs_ref, z_ref, at_ref, cv_ref, mod_ref,
                    dsk_ref, ng_ref, w_ref, fg_ref, o_ref, *, tile_off, final):
    d = xl_ref.shape[-1]
    b = pl.program_id(0)
    i = pl.program_id(1) + tile_off
    row = jnp.where(i == 0, CTX_ROW, b)
    x = xl_ref[0] if final else jnp.where(i == 0, xc_ref[0], xl_ref[0])
    gate = mod_ref[pl.ds(row, 1), :][:, 2 * d:]
    y = yf_ref[0].astype(F32) + yb_ref[0].astype(F32) + dsk_ref[...] * xs_ref[0].astype(F32)
    y = y * _silu(z_ref[0].astype(F32))
    y = y * lax.rsqrt(jnp.mean(y * y, axis=-1, keepdims=True) + EPS) * ng_ref[...]
    cat = jnp.concatenate([y.astype(BF16), at_ref[0], cv_ref[0]], axis=1)
    xn = x + gate * _dot(cat, w_ref[...])
    if final:
        xn = xn * lax.rsqrt(jnp.mean(xn * xn, axis=-1, keepdims=True) + EPS) * fg_ref[...]
    o_ref[0] = xn


def _outproj(ctx_src, lat_src, lat_off, yf, yb, xs, z, at, cv, mod_l, d_skip_exp, norm_g, w_out,
             final_g, final):
    nb, t, _ = yf.shape
    d = lat_src.shape[-1]
    tile_off = 1 if final else 0
    nt = t // TILE - tile_off
    if final:
        x_specs = [pl.BlockSpec((1, TILE, d), lambda b, i: (b, 0, 0)),
                   pl.BlockSpec((1, TILE, d), lambda b, i: (b, i + lat_off, 0))]
    else:
        x_specs = _row_sources(ctx_src, lat_src, lat_off, d)
    tile = lambda w: pl.BlockSpec((1, TILE, w), lambda b, i: (b, i + tile_off, 0))
    const = lambda shp: pl.BlockSpec(shp, lambda b, i: tuple(0 for _ in shp))
    return pl.pallas_call(
        functools.partial(_outproj_kernel, tile_off=tile_off, final=final),
        out_shape=jax.ShapeDtypeStruct((nb, nt * TILE, d), F32),
        grid=(nb, nt),
        in_specs=x_specs + [tile(SSD_WIDTH), tile(SSD_WIDTH), tile(SSD_WIDTH), tile(SSD_WIDTH),
                  tile(ATTN_WIDTH), tile(CM_WIDTH), const(mod_l.shape),
                  const((1, SSD_WIDTH)), const((1, SSD_WIDTH)), const(w_out.shape), const((1, d))],
        out_specs=pl.BlockSpec((1, TILE, d), lambda b, i: (b, i, 0)),
        compiler_params=_params(("arbitrary", "arbitrary")),
        name="out_proj_final" if final else "out_proj",
    )(ctx_src, lat_src, yf, yb, xs, z, at, cv, mod_l, d_skip_exp.reshape(1, SSD_WIDTH),
      norm_g.reshape(1, SSD_WIDTH), w_out, final_g.reshape(1, d))


def _expand_matrix():
    nh2 = 2 * SSD_HEADS
    e = np.zeros((DT_PAD, 2 * SSD_WIDTH), np.float32)
    for src in range(nh2):
        for term in range(3):
            e[term * nh2 + src, src * SSD_HEAD_DIM:(src + 1) * SSD_HEAD_DIM] = 1.0
    return jnp.asarray(e, BF16)


def _block_ones(width):
    idx = np.arange(width) // HEAD_DIM
    return jnp.asarray((idx[:, None] == idx[None, :]).astype(np.float32), BF16)


def _rope_tables(seq, n_ctx):
    n_rows = seq // GRID_W
    rows = jnp.repeat(jnp.arange(n_rows, dtype=F32), GRID_W)
    cols = jnp.tile(jnp.arange(GRID_W, dtype=F32), n_rows)
    inv = ROPE_BASE ** (-jnp.arange(ROPE_NF, dtype=F32) / ROPE_NF)
    ar, ac = rows[:, None] * inv, cols[:, None] * inv
    cs = jnp.concatenate([jnp.cos(ar), jnp.cos(ar), jnp.cos(ac), jnp.cos(ac)], axis=1)
    sn = jnp.concatenate([-jnp.sin(ar), jnp.sin(ar), -jnp.sin(ac), jnp.sin(ac)], axis=1)
    cs = jnp.concatenate([jnp.ones((n_ctx, HEAD_DIM), F32), cs], axis=0)
    sn = jnp.concatenate([jnp.zeros((n_ctx, HEAD_DIM), F32), sn], axis=0)
    return jnp.tile(cs, (1, LANES // HEAD_DIM)), jnp.tile(sn, (1, LANES // HEAD_DIM))


def kernel(x, c, ctx, c_ctx, norm_g, ada_w, ada_b, w_in, ssd_conv_w, ssd_conv_b, ssd_dt_bias, ssd_a_log, ssd_d, ssd_norm_g, q_norm_g, k_norm_g, cm_conv_w, cm_conv_b, cm_ln_g, cm_ln_b, cm_pw_w, cm_pw_b, w_out, final_norm_g):
    nb, seq, d = x.shape
    n_ctx = ctx.shape[1]
    depth = w_in.shape[0]
    assert n_ctx == TILE and seq % TILE == 0 and seq % GRID_W == 0 and nb <= CTX_ROW
    nh2 = 2 * SSD_HEADS
    z_end = SSD_WIDTH + SSD_XBC

    cvec = jnp.zeros((MOD_ROWS, d), F32).at[:nb].set(c).at[CTX_ROW].set(c_ctx)
    mod = _ada(cvec, ada_w, ada_b)
    cs, sn = _rope_tables(seq, n_ctx)
    e3 = _expand_matrix()
    bd = _block_ones(2 * LANES)

    src = (ctx, x, 0)
    for l in range(depth):
        final = l == depth - 1
        w = w_in[l]
        w_a = w[:, :z_end].astype(BF16)
        w_b = w[:, z_end + nh2:].astype(BF16)
        w_dt = jnp.pad(w[:, z_end:z_end + nh2], ((0, 0), (0, DT_PAD - nh2))).astype(BF16)
        pad_rows = lambda a, n: jnp.zeros((n, a.shape[1]), F32).at[:a.shape[0]].set(a)
        pad_dt = lambda a: jnp.zeros((1, DT_PAD), F32).at[0, :nh2].set(a)
        ssd_p = (pad_rows(ssd_conv_w[l], SUBLANES), ssd_conv_b[l].reshape(1, SSD_XBC),
                 pad_dt(ssd_dt_bias[l]), pad_dt(ssd_a_log[l]), e3)
        attn_p = (cs, sn, bd, jnp.tile(q_norm_g[l], ATTN_HEADS).reshape(1, ATTN_WIDTH),
                  jnp.tile(k_norm_g[l], ATTN_KV_HEADS).reshape(1, KV_WIDTH))
        cm_row = lambda a: a.reshape(1, CM_WIDTH)
        cm_p = (pad_rows(cm_conv_w[l], 4 * SUBLANES), cm_row(cm_conv_b[l]), cm_row(cm_ln_g[l]),
                cm_row(cm_ln_b[l]), cm_pw_w[l].astype(BF16), cm_row(cm_pw_b[l]))
        z, ga, xs, xdtf, xdtb, cm, bt, cum, cumt, qt, kk, vt, cv = _front(
            *src, mod[l], norm_g[l], w_a, w_b, w_dt, ssd_p, attn_p, cm_p)
        yf, yb = _ssd_scan(xdtf, xdtb, cm, bt, cum, cumt, n_ctx)
        at = _attention(qt, kk, vt, ga)
        out = _outproj(*src, yf, yb, xs, z, at, cv, mod[l], jnp.repeat(ssd_d[l], SSD_HEAD_DIM),
                       ssd_norm_g[l], w_out[l].astype(BF16), final_norm_g, final)
        src = (out, out, 1)
    return out
```

```python
import functools
import math

import numpy as np
import jax
import jax.numpy as jnp
from jax import lax
from jax.experimental import pallas as pl
from jax.experimental.pallas import tpu as pltpu

F32 = jnp.float32
BF16 = jnp.bfloat16

EPS = 1e-6
GRID_W = 64
ROPE_BASE = 10000.0

SSD_HEADS = 12
SSD_HEAD_DIM = 64
SSD_WIDTH = SSD_HEADS * SSD_HEAD_DIM
SSD_GROUPS = 2
SSD_STATE = 128
SSD_BC = SSD_GROUPS * SSD_STATE
SSD_XBC = SSD_WIDTH + 2 * SSD_BC
SSD_CONV_LEN = 5
SSD_CHUNK = 128
ATTN_HEADS = 12
ATTN_KV_HEADS = 4
ATTN_REP = ATTN_HEADS // ATTN_KV_HEADS
HEAD_DIM = 64
ATTN_WIDTH = ATTN_HEADS * HEAD_DIM
KV_WIDTH = ATTN_KV_HEADS * HEAD_DIM
ROPE_NF = HEAD_DIM // 4
V_AUG = HEAD_DIM + 16
ATTN_UNROLL = 15
CM_WIDTH = 512
CM_CONV_LEN = 31

LANES = 128
SUBLANES = 8
DT_PAD = LANES
TILE = 256
HALO = 16
CTX_ROW = 4
MOD_ROWS = 8
VMEM_LIMIT = 56 * 1024 * 1024

PROJ_OFFSETS = dict(z=0, xbc=SSD_WIDTH,
                    q=0, k=ATTN_WIDTH, v=ATTN_WIDTH + KV_WIDTH, ga=ATTN_WIDTH + 2 * KV_WIDTH,
                    ua=2 * ATTN_WIDTH + 2 * KV_WIDTH, ub=2 * ATTN_WIDTH + 2 * KV_WIDTH + CM_WIDTH,
                    gc=2 * ATTN_WIDTH + 2 * KV_WIDTH + 2 * CM_WIDTH)


def _dot(a, b):
    return jnp.dot(a, b, preferred_element_type=F32)


def _silu(x):
    return x * jax.nn.sigmoid(x)


def _split3(x):
    hi = x.astype(BF16)
    r1 = x - hi.astype(F32)
    mid = r1.astype(BF16)
    lo = (r1 - mid.astype(F32)).astype(BF16)
    return hi, mid, lo


def _params(sem):
    return pltpu.CompilerParams(dimension_semantics=sem, vmem_limit_bytes=VMEM_LIMIT)


def _ada_kernel(c_ref, w_ref, b_ref, o_ref):
    s = _silu(c_ref[...])
    s_hi = s.astype(BF16)
    s_lo = (s - s_hi.astype(F32)).astype(BF16)
    w = w_ref[0]
    w_hi = w.astype(BF16)
    w_lo = (w - w_hi.astype(F32)).astype(BF16)
    o_ref[0] = _dot(s_hi, w_hi) + _dot(s_hi, w_lo) + _dot(s_lo, w_hi) + b_ref[0]


def _ada(cvec, ada_w, ada_b):
    depth, d, n3 = ada_w.shape
    tn = 1024
    return pl.pallas_call(
        _ada_kernel,
        out_shape=jax.ShapeDtypeStruct((depth, MOD_ROWS, n3), F32),
        grid=(depth, n3 // tn),
        in_specs=[
            pl.BlockSpec((MOD_ROWS, d), lambda l, j: (0, 0)),
            pl.BlockSpec((1, d, tn), lambda l, j: (l, 0, j)),
            pl.BlockSpec((1, 1, tn), lambda l, j: (l, 0, j)),
        ],
        out_specs=pl.BlockSpec((1, MOD_ROWS, tn), lambda l, j: (l, 0, j)),
        compiler_params=_params(("arbitrary", "arbitrary")),
        name="ada_mod",
    )(cvec, ada_w, ada_b.reshape(depth, 1, n3))


def _row_sources(ctx_src, lat_src, lat_off, d):
    return [pl.BlockSpec((1, TILE, d), lambda b, i: (b, 0, 0)),
            pl.BlockSpec((1, TILE, d), lambda b, i: (b, jnp.maximum(i - 1, 0) + lat_off, 0))]


def _head_norm(x, bd, g):
    sq = x * x
    hi = sq.astype(BF16)
    lo = (sq - hi.astype(F32)).astype(BF16)
    parts = []
    w = bd.shape[0]
    for s in range(x.shape[1] // w):
        sl = slice(s * w, (s + 1) * w)
        parts.append(_dot(hi[:, sl], bd) + _dot(lo[:, sl], bd))
    ss = jnp.concatenate(parts, axis=1) if len(parts) > 1 else parts[0]
    return x * lax.rsqrt(ss * (1.0 / HEAD_DIM) + EPS) * g


def _rope(x, cs, sn):
    w = x.shape[1]
    lane = lax.broadcasted_iota(jnp.int32, x.shape, 1)
    first_half = (lane % (2 * ROPE_NF)) < ROPE_NF
    partner = jnp.where(first_half, pltpu.roll(x, w - ROPE_NF, axis=1), pltpu.roll(x, ROPE_NF, axis=1))
    return x * cs + partner * sn


def _ssd_prep(sext_ref, dt_raw, cw_ref, cb_ref, dtb_ref, alog_ref, e3_ref,
              xs_ref, xdtf_ref, xdtb_ref, cm_ref, bt_ref, cum_ref, cumt_ref):
    acc = jnp.broadcast_to(cb_ref[...], (TILE, SSD_XBC))
    xe = sext_ref[...]
    rows = xe.shape[0]
    for k in range(SSD_CONV_LEN):
        off = HALO - SSD_CONV_LEN // 2 + k
        acc = acc + pltpu.roll(xe, rows - off, axis=0)[:TILE, :] * cw_ref[k:k + 1, :]
    y = _silu(acc)
    xs = y[:, :SSD_WIDTH]
    bm = y[:, SSD_WIDTH:SSD_WIDTH + SSD_BC]
    xs_ref[0] = xs.astype(BF16)
    cm_ref[0] = y[:, SSD_WIDTH + SSD_BC:].astype(BF16)
    bt_ref[0] = bm.T.astype(BF16)

    lane = lax.broadcasted_iota(jnp.int32, (TILE, DT_PAD), 1)
    nh2 = 2 * SSD_HEADS
    z = dt_raw + dtb_ref[...]
    dt = jnp.maximum(z, 0.0) + jnp.log1p(jnp.exp(-jnp.abs(z)))
    dt = jnp.where(lane < nh2, dt, 0.0)
    a = dt * (-jnp.exp(alog_ref[...]))
    a = jnp.where(lane < nh2, a, 0.0)

    d_hi, d_mid, d_lo = _split3(dt)
    d3 = (d_hi.astype(F32) + pltpu.roll(d_mid.astype(F32), nh2, axis=1)
          + pltpu.roll(d_lo.astype(F32), 2 * nh2, axis=1)).astype(BF16)
    dexp = _dot(d3, e3_ref[...])
    xdtf_ref[0] = (xs * dexp[:, :SSD_WIDTH]).astype(BF16)
    xdtb_ref[0] = (xs * dexp[:, SSD_WIDTH:]).astype(BF16)

    r = lax.broadcasted_iota(jnp.int32, (SSD_CHUNK, 3 * SSD_CHUNK), 0)
    c = lax.broadcasted_iota(jnp.int32, (SSD_CHUNK, 3 * SSD_CHUNK), 1) % SSD_CHUNK
    tri_l = jnp.where(c <= r, 1.0, 0.0).astype(BF16)
    tri_u = jnp.where(c >= r, 1.0, 0.0).astype(BF16)
    lane_c = lax.broadcasted_iota(jnp.int32, (SSD_CHUNK, DT_PAD), 1)
    cums = []
    for ch in range(TILE // SSD_CHUNK):
        a_c = a[ch * SSD_CHUNK:(ch + 1) * SSD_CHUNK, :]
        a3 = jnp.concatenate(_split3(a_c), axis=0)
        cums.append(jnp.where(lane_c < SSD_HEADS, _dot(tri_l, a3), _dot(tri_u, a3)))
    cum = jnp.concatenate(cums, axis=0) * math.log2(math.e)
    cum_ref[0] = cum
    cumt_ref[0] = cum.T[:cumt_ref.shape[1], :]


def _attn_prep(q, k, v, cs_ref, sn_ref, bd_ref, qg_ref, kg_ref, qt_ref, ko_ref, vt_ref):
    bd = bd_ref[...]
    cs = jnp.concatenate([cs_ref[...]] * (ATTN_WIDTH // LANES), axis=1)
    sn = jnp.concatenate([sn_ref[...]] * (ATTN_WIDTH // LANES), axis=1)
    q = _rope(_head_norm(q, bd, qg_ref[...]), cs, sn)
    q = q * (HEAD_DIM ** -0.5 * math.log2(math.e))
    qt_ref[0] = q.T.astype(BF16)
    k = _rope(_head_norm(k, bd, kg_ref[...]), cs[:, :KV_WIDTH], sn[:, :KV_WIDTH])
    ko_ref[0, 0] = k.astype(BF16)
    vt = v.T.astype(BF16)
    ones_blk = jnp.where(lax.broadcasted_iota(jnp.int32, (V_AUG - HEAD_DIM, TILE), 0) == 0,
                         1.0, 0.0).astype(BF16)
    parts = []
    for g in range(ATTN_KV_HEADS):
        parts += [vt[g * HEAD_DIM:(g + 1) * HEAD_DIM, :], ones_blk]
    vt_ref[0, 0] = jnp.concatenate(parts, axis=0)


def _conformer(cext_ref, sh_ref, gc, cw_ref, cb_ref, lg_ref, lb_ref, pw_ref, pb_ref, o_ref):
    first = HALO - CM_CONV_LEN // 2
    ce = cext_ref[...]
    for ph in range(SUBLANES):
        sh_ref[ph] = pltpu.roll(ce, ce.shape[0] - (first + ph), axis=0)[:sh_ref.shape[1], :]
    acc = jnp.broadcast_to(cb_ref[...], (TILE, CM_WIDTH))
    for k in range(CM_CONV_LEN):
        base = k // SUBLANES * SUBLANES
        acc = acc + sh_ref[k % SUBLANES, base:base + TILE, :] * cw_ref[k:k + 1, :]
    mu = jnp.mean(acc, axis=-1, keepdims=True)
    xc = acc - mu
    var = jnp.mean(xc * xc, axis=-1, keepdims=True)
    u = _silu(xc * lax.rsqrt(var + EPS) * lg_ref[...] + lb_ref[...])
    u = _dot(u.astype(BF16), pw_ref[...]) + pb_ref[...]
    o_ref[0] = (u * _silu(gc)).astype(o_ref.dtype)


def _front_kernel(xc_ref, xl_ref, xp_ref, xn_ref, mod_ref, g_ref, wa_ref, wb_ref, wd_ref,
                  scw_ref, scb_ref, dtb_ref, alog_ref, e3_ref,
                  cs_ref, sn_ref, bd_ref, qg_ref, kg_ref,
                  ccw_ref, ccb_ref, lg_ref, lb_ref, pw_ref, pb_ref,
                  z_ref, ga_ref, xs_ref, xdtf_ref, xdtb_ref, cm_ref, bt_ref, cum_ref, cumt_ref,
                  qt_ref, ko_ref, vt_ref, cv_ref,
                  sext_ref, cext_ref, sh_ref):
    d = xl_ref.shape[-1]
    b = pl.program_id(0)
    i = pl.program_id(1)
    nt = pl.num_programs(1)
    row = jnp.where(i == 0, CTX_ROW, b)
    m = mod_ref[pl.ds(row, 1), :]
    shift = m[:, :d]
    scale1 = 1.0 + m[:, d:2 * d]
    prev_ok = i >= 2
    next_ok = jnp.logical_and(i != 0, i != nt - 1)

    def mod_norm(x):
        ms = jnp.mean(x * x, axis=-1, keepdims=True)
        return ((x * lax.rsqrt(ms + EPS) * g_ref[...]) * scale1 + shift).astype(BF16)

    x = jnp.where(i == 0, xc_ref[0], xl_ref[0])
    hb = jnp.concatenate([mod_norm(xp_ref[0]), mod_norm(x), mod_norm(xn_ref[0])], axis=0)
    hm = hb[HALO:HALO + TILE]

    def halo_masked(v):
        r = lax.broadcasted_iota(jnp.int32, v.shape, 0)
        ok = jnp.logical_and(jnp.logical_or(r >= HALO, prev_ok), jnp.logical_or(r < HALO + TILE, next_ok))
        return jnp.where(ok, v, 0.0)

    o = PROJ_OFFSETS
    ua = _dot(hb, wb_ref[:, o["ua"]:o["ua"] + CM_WIDTH])
    ub = _dot(hb, wb_ref[:, o["ub"]:o["ub"] + CM_WIDTH])
    cext_ref[...] = halo_masked(ua * jax.nn.sigmoid(ub))
    sext_ref[...] = halo_masked(_dot(hb, wa_ref[:, o["xbc"]:o["xbc"] + SSD_XBC]))
    dt_raw = _dot(hm, wd_ref[...])
    q = _dot(hm, wb_ref[:, o["q"]:o["q"] + ATTN_WIDTH])
    k = _dot(hm, wb_ref[:, o["k"]:o["k"] + KV_WIDTH])
    v = _dot(hm, wb_ref[:, o["v"]:o["v"] + KV_WIDTH])
    gc = _dot(hm, wb_ref[:, o["gc"]:o["gc"] + CM_WIDTH])
    z_ref[0] = _dot(hm, wa_ref[:, o["z"]:o["z"] + SSD_WIDTH]).astype(BF16)
    ga_ref[0] = _dot(hm, wb_ref[:, o["ga"]:o["ga"] + ATTN_WIDTH]).astype(BF16)
    _attn_prep(q, k, v, cs_ref, sn_ref, bd_ref, qg_ref, kg_ref, qt_ref, ko_ref, vt_ref)
    _ssd_prep(sext_ref, dt_raw, scw_ref, scb_ref, dtb_ref, alog_ref, e3_ref,
              xs_ref, xdtf_ref, xdtb_ref, cm_ref, bt_ref, cum_ref, cumt_ref)
    _conformer(cext_ref, sh_ref, gc, ccw_ref, ccb_ref, lg_ref, lb_ref, pw_ref, pb_ref, cv_ref)


def _front(ctx_src, lat_src, lat_off, mod_l, g, w_a, w_b, w_dt, ssd_p, attn_p, cm_p):
    nb, lat_rows, d = lat_src.shape
    nt = 1 + lat_rows // TILE - lat_off
    t = nt * TILE
    hpt = TILE // HALO
    n_hb = lat_rows // HALO

    def lat_tile(i):
        return jnp.maximum(i - 1, 0) + lat_off

    tok = lambda w, dt_: jax.ShapeDtypeStruct((nb, t, w), dt_)
    tile = lambda w: pl.BlockSpec((1, TILE, w), lambda b, i: (b, i, 0))
    const = lambda a: pl.BlockSpec(a.shape, lambda b, i: tuple(0 for _ in a.shape))
    consts = (mod_l, g.reshape(1, d), w_a, w_b, w_dt) + ssd_p
    attn_consts = attn_p[2:]
    ctr = 32
    return pl.pallas_call(
        _front_kernel,
        out_shape=(tok(SSD_WIDTH, BF16), tok(ATTN_WIDTH, BF16),
                   tok(SSD_WIDTH, BF16), tok(SSD_WIDTH, BF16), tok(SSD_WIDTH, BF16), tok(SSD_BC, BF16),
                   jax.ShapeDtypeStruct((nb, SSD_BC, t), BF16), tok(DT_PAD, F32),
                   jax.ShapeDtypeStruct((nb, ctr, t), F32),
                   jax.ShapeDtypeStruct((nb, ATTN_WIDTH, t), BF16),
                   jax.ShapeDtypeStruct((nb, nt, TILE, KV_WIDTH), BF16),
                   jax.ShapeDtypeStruct((nb, nt, ATTN_KV_HEADS * V_AUG, TILE), BF16),
                   tok(CM_WIDTH, BF16)),
        grid=(nb, nt),
        in_specs=_row_sources(ctx_src, lat_src, lat_off, d) + [
            pl.BlockSpec((1, HALO, d), lambda b, i: (b, jnp.maximum(lat_tile(i) * hpt - 1, 0), 0)),
            pl.BlockSpec((1, HALO, d), lambda b, i: (b, jnp.minimum((lat_tile(i) + 1) * hpt, n_hb - 1), 0)),
        ] + [const(a) for a in consts] + [
            pl.BlockSpec((TILE, LANES), lambda b, i: (i, 0)),
            pl.BlockSpec((TILE, LANES), lambda b, i: (i, 0)),
        ] + [const(a) for a in attn_consts] + [const(a) for a in cm_p],
        out_specs=(tile(SSD_WIDTH), tile(ATTN_WIDTH),
                   tile(SSD_WIDTH), tile(SSD_WIDTH), tile(SSD_WIDTH), tile(SSD_BC),
                   pl.BlockSpec((1, SSD_BC, TILE), lambda b, i: (b, 0, i)), tile(DT_PAD),
                   pl.BlockSpec((1, ctr, TILE), lambda b, i: (b, 0, i)),
                   pl.BlockSpec((1, ATTN_WIDTH, TILE), lambda b, i: (b, 0, i)),
                   pl.BlockSpec((1, 1, TILE, KV_WIDTH), lambda b, i: (b, i, 0, 0)),
                   pl.BlockSpec((1, 1, ATTN_KV_HEADS * V_AUG, TILE), lambda b, i: (b, i, 0, 0)),
                   tile(CM_WIDTH)),
        scratch_shapes=[pltpu.VMEM((TILE + 2 * HALO, SSD_XBC), F32),
                        pltpu.VMEM((TILE + 2 * HALO, CM_WIDTH), F32),
                        pltpu.VMEM((SUBLANES, TILE + (CM_CONV_LEN - 1) // SUBLANES * SUBLANES, CM_WIDTH), F32)],
        compiler_params=_params(("arbitrary", "arbitrary")),
        name="front",
    )(ctx_src, lat_src, lat_src, lat_src, *consts, attn_p[0], attn_p[1], *attn_consts, *cm_p)


def _ssd_direction(xdt, cm, bt, cum, cumt, h_ref, col0, fwd):
    n = SSD_CHUNK
    li = lax.broadcasted_iota(jnp.int32, (n, n), 0)
    si = lax.broadcasted_iota(jnp.int32, (n, n), 1)
    mask = (li >= si) if fwd else (si >= li)
    lo_half = si < SSD_HEAD_DIM
    tot_row = n - 1 if fwd else 0
    hprev = h_ref[...]
    hb = hprev.astype(BF16)
    gw = SSD_WIDTH // SSD_GROUPS
    pairs = SSD_HEADS // SSD_GROUPS // 2
    y_slabs = []
    h_new = []
    for g in range(SSD_GROUPS):
        cg = cm[:, g * SSD_STATE:(g + 1) * SSD_STATE]
        btg = bt[g * SSD_STATE:(g + 1) * SSD_STATE, :]
        cb = _dot(cg, btg)
        yoff = _dot(cg, hb[:, g * gw:(g + 1) * gw])
        xw_slabs = []
        etot_slabs = []
        for jj in range(pairs):
            j = g * pairs + jj
            h0 = col0 + 2 * j
            cb0 = jnp.broadcast_to(cum[:, h0:h0 + 1], (n, n))
            cb1 = jnp.broadcast_to(cum[:, h0 + 1:h0 + 2], (n, n))
            l0 = jnp.exp2(jnp.where(mask, cb0 - cumt[h0:h0 + 1, :], -jnp.inf))
            l1 = jnp.exp2(jnp.where(mask, cb1 - cumt[h0 + 1:h0 + 2, :], -jnp.inf))
            mm = jnp.concatenate([(cb * l0).astype(BF16), (cb * l1).astype(BF16)], axis=1)
            slab = xdt[:, j * LANES:(j + 1) * LANES]
            zero = jnp.zeros_like(slab)
            ww = jnp.concatenate([jnp.where(lo_half, slab, zero), jnp.where(lo_half, zero, slab)], axis=0)
            ydiag = _dot(mm, ww)
            csel = jnp.where(lo_half, cb0, cb1)
            tot = csel[tot_row:tot_row + 1, :]
            y_slabs.append(ydiag + yoff[:, jj * LANES:(jj + 1) * LANES] * jnp.exp2(csel))
            xw_slabs.append((slab.astype(F32) * jnp.exp2(tot - csel)).astype(BF16))
            etot_slabs.append(jnp.exp2(tot))
        xw = jnp.concatenate(xw_slabs, axis=1)
        etot = jnp.concatenate(etot_slabs, axis=1)
        h_new.append(hprev[:, g * gw:(g + 1) * gw] * etot + _dot(btg, xw))
    h_ref[...] = jnp.concatenate(h_new, axis=1)
    return jnp.concatenate(y_slabs, axis=1)


def _ssd_scan_kernel(xf_ref, cf_ref, btf_ref, cumf_ref, cumtf_ref,
                     xb_ref, cb_ref, btb_ref, cumb_ref, cumtb_ref,
                     yf_ref, yb_ref, hf_ref, hb_ref):
    @pl.when(pl.program_id(1) == 0)
    def _():
        hf_ref[...] = jnp.zeros_like(hf_ref)
        hb_ref[...] = jnp.zeros_like(hb_ref)

    n_ch = TILE // SSD_CHUNK
    for ch in range(n_ch):
        r = slice(ch * SSD_CHUNK, (ch + 1) * SSD_CHUNK)
        yf_ref[0, r, :] = _ssd_direction(xf_ref[0, r, :], cf_ref[0, r, :], btf_ref[0, :, r], cumf_ref[0, r, :],
                                         cumtf_ref[0, :, r], hf_ref, 0, True).astype(yf_ref.dtype)
    for ch in reversed(range(n_ch)):
        r = slice(ch * SSD_CHUNK, (ch + 1) * SSD_CHUNK)
        yb_ref[0, r, :] = _ssd_direction(xb_ref[0, r, :], cb_ref[0, r, :], btb_ref[0, :, r], cumb_ref[0, r, :],
                                         cumtb_ref[0, :, r], hb_ref, SSD_HEADS, False).astype(yb_ref.dtype)


def _ssd_scan(xdtf, xdtb, cm, bt, cum, cumt, n_ctx):
    nb, t, _ = xdtf.shape
    nt = t // TILE
    assert n_ctx == TILE
    ctr = cumt.shape[1]

    def bmap(s):
        return jnp.where(s == 0, 0, nt - s)

    fmap = lambda s: s
    rows = lambda w, tm: pl.BlockSpec((1, TILE, w), lambda b, s: (b, tm(s), 0))
    cols = lambda h, tm: pl.BlockSpec((1, h, TILE), lambda b, s: (b, 0, tm(s)))
    side = lambda tm: [rows(SSD_WIDTH, tm), rows(SSD_BC, tm), cols(SSD_BC, tm),
                       rows(DT_PAD, tm), cols(ctr, tm)]
    return pl.pallas_call(
        _ssd_scan_kernel,
        out_shape=(jax.ShapeDtypeStruct((nb, t, SSD_WIDTH), BF16),
                   jax.ShapeDtypeStruct((nb, t, SSD_WIDTH), BF16)),
        grid=(nb, nt),
        in_specs=side(fmap) + side(bmap),
        out_specs=(rows(SSD_WIDTH, fmap), rows(SSD_WIDTH, bmap)),
        scratch_shapes=[pltpu.VMEM((SSD_STATE, SSD_WIDTH), F32), pltpu.VMEM((SSD_STATE, SSD_WIDTH), F32)],
        compiler_params=_params(("arbitrary", "arbitrary")),
        name="ssd_scan",
    )(xdtf, cm, bt, cum, cumt, xdtb, cm, bt, cum, cumt)


def _attn_kernel(qt_ref, k_ref, vt_ref, ga_ref, o_ref, qp_ref, s_ref, p_ref, acc_ref, ot_ref):
    i = pl.program_id(1)
    n_chunks = k_ref.shape[1]
    gq = ATTN_REP * HEAD_DIM
    wq = ATTN_REP * TILE
    groups = range(ATTN_KV_HEADS)
    zero = jnp.zeros((HEAD_DIM, TILE), BF16)
    for g in groups:
        upper = (g * HEAD_DIM) % LANES != 0
        for r in range(ATTN_REP):
            qh = qt_ref[0, g * gq + r * HEAD_DIM:g * gq + (r + 1) * HEAD_DIM, :]
            qp_ref[g, :, r * TILE:(r + 1) * TILE] = jnp.concatenate([zero, qh] if upper else [qh, zero], axis=0)
    acc_ref[...] = jnp.zeros_like(acc_ref)

    def qk_stage(g, c):
        lane_tile = (g * HEAD_DIM) // LANES
        kc = k_ref[0, c, :, lane_tile * LANES:(lane_tile + 1) * LANES]
        s = _dot(kc, qp_ref[g])
        s_ref[g] = s
        return jnp.max(s, axis=0, keepdims=True)

    def pv_stage(g, c, a_prev):
        vc = vt_ref[0, c, g * V_AUG:(g + 1) * V_AUG, :]
        acc_ref[g] = acc_ref[g] * a_prev + _dot(vc, p_ref[g])

    def step(c, carry, do_pv, do_qk):
        ms, a_prevs, cmaxs = carry
        new_m, new_a, new_cmax = [], [], []
        for g in groups:
            if do_pv:
                pv_stage(g, c - 1, a_prevs[g])
            m_new = jnp.maximum(ms[g], cmaxs[g])
            p_ref[g] = jnp.exp2(s_ref[g] - m_new).astype(BF16)
            new_cmax.append(qk_stage(g, c + 1) if do_qk else cmaxs[g])
            new_m.append(m_new)
            new_a.append(jnp.exp2(ms[g] - m_new))
        return tuple(new_m), tuple(new_a), tuple(new_cmax)

    def run(nk):
        carry = (tuple(jnp.full((1, wq), -jnp.inf, F32) for _ in groups),
                 tuple(jnp.ones((1, wq), F32) for _ in groups),
                 tuple(qk_stage(g, 0) for g in groups))
        carry = step(0, carry, False, nk > 1)
        if nk > 2:
            carry = lax.fori_loop(1, nk - 1, lambda c, cr: step(c, cr, True, True), carry,
                                  unroll=ATTN_UNROLL)
        if nk > 1:
            carry = step(nk - 1, carry, True, False)
        for g in groups:
            pv_stage(g, nk - 1, carry[1][g])
            acc = acc_ref[g]
            ot = acc[:HEAD_DIM, :] / acc[HEAD_DIM:HEAD_DIM + 1, :]
            for r in range(ATTN_REP):
                ot_ref[g * gq + r * HEAD_DIM:g * gq + (r + 1) * HEAD_DIM, :] = ot[:, r * TILE:(r + 1) * TILE]
        o_ref[0] = (ot_ref[...].T * _silu(ga_ref[0].astype(F32))).astype(o_ref.dtype)

    @pl.when(i == 0)
    def _():
        run(1)

    @pl.when(i != 0)
    def _():
        run(n_chunks)


def _attention(qt, kk, vt, ga):
    nb, t, _ = ga.shape
    nt = t // TILE
    return pl.pallas_call(
        _attn_kernel,
        out_shape=jax.ShapeDtypeStruct((nb, t, ATTN_WIDTH), BF16),
        grid=(nb, nt),
        in_specs=[pl.BlockSpec((1, ATTN_WIDTH, TILE), lambda b, i: (b, 0, i)),
                  pl.BlockSpec((1, nt, TILE, KV_WIDTH), lambda b, i: (b, 0, 0, 0)),
                  pl.BlockSpec((1, nt, ATTN_KV_HEADS * V_AUG, TILE), lambda b, i: (b, 0, 0, 0)),
                  pl.BlockSpec((1, TILE, ATTN_WIDTH), lambda b, i: (b, i, 0))],
        out_specs=pl.BlockSpec((1, TILE, ATTN_WIDTH), lambda b, i: (b, i, 0)),
        scratch_shapes=[pltpu.VMEM((ATTN_KV_HEADS, LANES, ATTN_REP * TILE), BF16),
                        pltpu.VMEM((ATTN_KV_HEADS, TILE, ATTN_REP * TILE), F32),
                        pltpu.VMEM((ATTN_KV_HEADS, TILE, ATTN_REP * TILE), BF16),
                        pltpu.VMEM((ATTN_KV_HEADS, V_AUG, ATTN_REP * TILE), F32),
                        pltpu.VMEM((ATTN_WIDTH, TILE), F32)],
        compiler_params=_params(("arbitrary", "arbitrary")),
        name="attention",
    )(qt, kk, vt, ga)


def _outproj_kernel(xc_ref, xl_ref, yf_ref, yb_ref, xs_ref, z_ref, at_ref, cv_ref, mod_ref,
                    dsk_ref, ng_ref, w_ref, fg_ref, o_ref, *, tile_off, final):
    d = xl_ref.shape[-1]
    b = pl.program_id(0)
    i = pl.program_id(1) + tile_off
    row = jnp.where(i == 0, CTX_ROW, b)
    x = xl_ref[0] if final else jnp.where(i == 0, xc_ref[0], xl_ref[0])
    gate = mod_ref[pl.ds(row, 1), :][:, 2 * d:]
    y = yf_ref[0].astype(F32) + yb_ref[0].astype(F32) + dsk_ref[...] * xs_ref[0].astype(F32)
    y = y * _silu(z_ref[0].astype(F32))
    y = y * lax.rsqrt(jnp.mean(y * y, axis=-1, keepdims=True) + EPS) * ng_ref[...]
    cat = jnp.concatenate([y.astype(BF16), at_ref[0], cv_ref[0]], axis=1)
    xn = x + gate * _dot(cat, w_ref[...])
    if final:
        xn = xn * lax.rsqrt(jnp.mean(xn * xn, axis=-1, keepdims=True) + EPS) * fg_ref[...]
    o_ref[0] = xn


def _outproj(ctx_src, lat_src, lat_off, yf, yb, xs, z, at, cv, mod_l, d_skip_exp, norm_g, w_out,
             final_g, final):
    nb, t, _ = yf.shape
    d = lat_src.shape[-1]
    tile_off = 1 if final else 0
    nt = t // TILE - tile_off
    if final:
        x_specs = [pl.BlockSpec((1, TILE, d), lambda b, i: (b, 0, 0)),
                   pl.BlockSpec((1, TILE, d), lambda b, i: (b, i + lat_off, 0))]
    else:
        x_specs = _row_sources(ctx_src, lat_src, lat_off, d)
    tile = lambda w: pl.BlockSpec((1, TILE, w), lambda b, i: (b, i + tile_off, 0))
    const = lambda shp: pl.BlockSpec(shp, lambda b, i: tuple(0 for _ in shp))
    return pl.pallas_call(
        functools.partial(_outproj_kernel, tile_off=tile_off, final=final),
        out_shape=jax.ShapeDtypeStruct((nb, nt * TILE, d), F32),
        grid=(nb, nt),
        in_specs=x_specs + [tile(SSD_WIDTH), tile(SSD_WIDTH), tile(SSD_WIDTH), tile(SSD_WIDTH),
                  tile(ATTN_WIDTH), tile(CM_WIDTH), const(mod_l.shape),
                  const((1, SSD_WIDTH)), const((1, SSD_WIDTH)), const(w_out.shape), const((1, d))],
        out_specs=pl.BlockSpec((1, TILE, d), lambda b, i: (b, i, 0)),
        compiler_params=_params(("arbitrary", "arbitrary")),
        name="out_proj_final" if final else "out_proj",
    )(ctx_src, lat_src, yf, yb, xs, z, at, cv, mod_l, d_skip_exp.reshape(1, SSD_WIDTH),
      norm_g.reshape(1, SSD_WIDTH), w_out, final_g.reshape(1, d))


def _expand_matrix():
    nh2 = 2 * SSD_HEADS
    e = np.zeros((DT_PAD, 2 * SSD_WIDTH), np.float32)
    for src in range(nh2):
        for term in range(3):
            e[term * nh2 + src, src * SSD_HEAD_DIM:(src + 1) * SSD_HEAD_DIM] = 1.0
    return jnp.asarray(e, BF16)


def _block_ones(width):
    idx = np.arange(width) // HEAD_DIM
    return jnp.asarray((idx[:, None] == idx[None, :]).astype(np.float32), BF16)


def _rope_tables(seq, n_ctx):
    n_rows = seq // GRID_W
    rows = jnp.repeat(jnp.arange(n_rows, dtype=F32), GRID_W)
    cols = jnp.tile(jnp.arange(GRID_W, dtype=F32), n_rows)
    inv = ROPE_BASE ** (-jnp.arange(ROPE_NF, dtype=F32) / ROPE_NF)
    ar, ac = rows[:, None] * inv, cols[:, None] * inv
    cs = jnp.concatenate([jnp.cos(ar), jnp.cos(ar), jnp.cos(ac), jnp.cos(ac)], axis=1)
    sn = jnp.concatenate([-jnp.sin(ar), jnp.sin(ar), -jnp.sin(ac), jnp.sin(ac)], axis=1)
    cs = jnp.concatenate([jnp.ones((n_ctx, HEAD_DIM), F32), cs], axis=0)
    sn = jnp.concatenate([jnp.zeros((n_ctx, HEAD_DIM), F32), sn], axis=0)
    return jnp.tile(cs, (1, LANES // HEAD_DIM)), jnp.tile(sn, (1, LANES // HEAD_DIM))


def kernel(x, c, ctx, c_ctx, norm_g, ada_w, ada_b, w_in, ssd_conv_w, ssd_conv_b, ssd_dt_bias, ssd_a_log, ssd_d, ssd_norm_g, q_norm_g, k_norm_g, cm_conv_w, cm_conv_b, cm_ln_g, cm_ln_b, cm_pw_w, cm_pw_b, w_out, final_norm_g):
    nb, seq, d = x.shape
    n_ctx = ctx.shape[1]
    depth = w_in.shape[0]
    assert n_ctx == TILE and seq % TILE == 0 and seq % GRID_W == 0 and nb <= CTX_ROW
    nh2 = 2 * SSD_HEADS
    z_end = SSD_WIDTH + SSD_XBC

    cvec = jnp.zeros((MOD_ROWS, d), F32).at[:nb].set(c).at[CTX_ROW].set(c_ctx)
    mod = _ada(cvec, ada_w, ada_b)
    cs, sn = _rope_tables(seq, n_ctx)
    e3 = _expand_matrix()
    bd = _block_ones(2 * LANES)

    src = (ctx, x, 0)
    for l in range(depth):
        final = l == depth - 1
        w = w_in[l]
        w_a = w[:, :z_end].astype(BF16)
        w_b = w[:, z_end + nh2:].astype(BF16)
        w_dt = jnp.pad(w[:, z_end:z_end + nh2], ((0, 0), (0, DT_PAD - nh2))).astype(BF16)
        pad_rows = lambda a, n: jnp.zeros((n, a.shape[1]), F32).at[:a.shape[0]].set(a)
        pad_dt = lambda a: jnp.zeros((1, DT_PAD), F32).at[0, :nh2].set(a)
        ssd_p = (pad_rows(ssd_conv_w[l], SUBLANES), ssd_conv_b[l].reshape(1, SSD_XBC),
                 pad_dt(ssd_dt_bias[l]), pad_dt(ssd_a_log[l]), e3)
        attn_p = (cs, sn, bd, jnp.tile(q_norm_g[l], ATTN_HEADS).reshape(1, ATTN_WIDTH),
                  jnp.tile(k_norm_g[l], ATTN_KV_HEADS).reshape(1, KV_WIDTH))
        cm_row = lambda a: a.reshape(1, CM_WIDTH)
        cm_p = (pad_rows(cm_conv_w[l], 4 * SUBLANES), cm_row(cm_conv_b[l]), cm_row(cm_ln_g[l]),
                cm_row(cm_ln_b[l]), cm_pw_w[l].astype(BF16), cm_row(cm_pw_b[l]))
        z, ga, xs, xdtf, xdtb, cm, bt, cum, cumt, qt, kk, vt, cv = _front(
            *src, mod[l], norm_g[l], w_a, w_b, w_dt, ssd_p, attn_p, cm_p)
        yf, yb = _ssd_scan(xdtf, xdtb, cm, bt, cum, cumt, n_ctx)
        at = _attention(qt, kk, vt, ga)
        out = _outproj(*src, yf, yb, xs, z, at, cv, mod[l], jnp.repeat(ssd_d[l], SSD_HEAD_DIM),
                       ssd_norm_g[l], w_out[l].astype(BF16), final_norm_g, final)
        src = (out, out, 1)
    return out
```

```python
import functools
import math

import numpy as np
import jax
import jax.numpy as jnp
from jax import lax
from jax.experimental import pallas as pl
from jax.experimental.pallas import tpu as pltpu

F32 = jnp.float32
BF16 = jnp.bfloat16

EPS = 1e-6
GRID_W = 64
ROPE_BASE = 10000.0

SSD_HEADS = 12
SSD_HEAD_DIM = 64
SSD_WIDTH = SSD_HEADS * SSD_HEAD_DIM
SSD_GROUPS = 2
SSD_STATE = 128
SSD_BC = SSD_GROUPS * SSD_STATE
SSD_XBC = SSD_WIDTH + 2 * SSD_BC
SSD_CONV_LEN = 5
SSD_CHUNK = 128
ATTN_HEADS = 12
ATTN_KV_HEADS = 4
ATTN_REP = ATTN_HEADS // ATTN_KV_HEADS
HEAD_DIM = 64
ATTN_WIDTH = ATTN_HEADS * HEAD_DIM
KV_WIDTH = ATTN_KV_HEADS * HEAD_DIM
ROPE_NF = HEAD_DIM // 4
V_AUG = HEAD_DIM + 16
ATTN_UNROLL = 15
CM_WIDTH = 512
CM_CONV_LEN = 31

LANES = 128
SUBLANES = 8
DT_PAD = LANES
TILE = 256
HALO = 16
CTX_ROW = 4
MOD_ROWS = 8
VMEM_LIMIT = 56 * 1024 * 1024

PROJ_OFFSETS = dict(z=0, xbc=SSD_WIDTH,
                    q=0, k=ATTN_WIDTH, v=ATTN_WIDTH + KV_WIDTH, ga=ATTN_WIDTH + 2 * KV_WIDTH,
                    ua=2 * ATTN_WIDTH + 2 * KV_WIDTH, ub=2 * ATTN_WIDTH + 2 * KV_WIDTH + CM_WIDTH,
                    gc=2 * ATTN_WIDTH + 2 * KV_WIDTH + 2 * CM_WIDTH)
W_DT_START = SSD_WIDTH + SSD_XBC
W_B_START = W_DT_START + 2 * SSD_HEADS


def _dot(a, b):
    return jnp.dot(a, b, preferred_element_type=F32)


def _silu(x):
    return x * jax.nn.sigmoid(x)


def _split3(x):
    hi = x.astype(BF16)
    r1 = x - hi.astype(F32)
    mid = r1.astype(BF16)
    lo = (r1 - mid.astype(F32)).astype(BF16)
    return hi, mid, lo


def _params(sem):
    return pltpu.CompilerParams(dimension_semantics=sem, vmem_limit_bytes=VMEM_LIMIT)


def _ada_kernel(c_ref, w_ref, b_ref, o_ref):
    s = _silu(c_ref[...])
    s_hi = s.astype(BF16)
    s_lo = (s - s_hi.astype(F32)).astype(BF16)
    w = w_ref[0]
    w_hi = w.astype(BF16)
    w_lo = (w - w_hi.astype(F32)).astype(BF16)
    o_ref[0] = _dot(s_hi, w_hi) + _dot(s_hi, w_lo) + _dot(s_lo, w_hi) + b_ref[0]


def _ada(cvec, ada_w, ada_b):
    depth, d, n3 = ada_w.shape
    tn = 1024
    return pl.pallas_call(
        _ada_kernel,
        out_shape=jax.ShapeDtypeStruct((depth, MOD_ROWS, n3), F32),
        grid=(depth, n3 // tn),
        in_specs=[
            pl.BlockSpec((MOD_ROWS, d), lambda l, j: (0, 0)),
            pl.BlockSpec((1, d, tn), lambda l, j: (l, 0, j)),
            pl.BlockSpec((1, 1, tn), lambda l, j: (l, 0, j)),
        ],
        out_specs=pl.BlockSpec((1, MOD_ROWS, tn), lambda l, j: (l, 0, j)),
        compiler_params=_params(("arbitrary", "arbitrary")),
        name="ada_mod",
    )(cvec, ada_w, ada_b.reshape(depth, 1, n3))


def _row_sources(ctx_src, lat_src, lat_off, d):
    return [pl.BlockSpec((1, TILE, d), lambda b, i: (b, 0, 0)),
            pl.BlockSpec((1, TILE, d), lambda b, i: (b, jnp.maximum(i - 1, 0) + lat_off, 0))]


def _head_norm(x, bd, g):
    sq = x * x
    hi = sq.astype(BF16)
    lo = (sq - hi.astype(F32)).astype(BF16)
    parts = []
    w = bd.shape[0]
    for s in range(x.shape[1] // w):
        sl = slice(s * w, (s + 1) * w)
        parts.append(_dot(hi[:, sl], bd) + _dot(lo[:, sl], bd))
    ss = jnp.concatenate(parts, axis=1) if len(parts) > 1 else parts[0]
    return x * lax.rsqrt(ss * (1.0 / HEAD_DIM) + EPS) * g


def _rope(x, cs, sn):
    w = x.shape[1]
    lane = lax.broadcasted_iota(jnp.int32, x.shape, 1)
    first_half = (lane % (2 * ROPE_NF)) < ROPE_NF
    partner = jnp.where(first_half, pltpu.roll(x, w - ROPE_NF, axis=1), pltpu.roll(x, ROPE_NF, axis=1))
    return x * cs + partner * sn


def _ssd_prep(sext_ref, dt_raw, cw_ref, cb_ref, dtb_ref, alog_ref, e3_ref,
              xs_ref, xdtf_ref, xdtb_ref, cm_ref, bt_ref, cum_ref, cumt_ref):
    acc = jnp.broadcast_to(cb_ref[...], (TILE, SSD_XBC))
    xe = sext_ref[...]
    rows = xe.shape[0]
    for k in range(SSD_CONV_LEN):
        off = HALO - SSD_CONV_LEN // 2 + k
        acc = acc + pltpu.roll(xe, rows - off, axis=0)[:TILE, :] * cw_ref[k:k + 1, :]
    y = _silu(acc)
    xs = y[:, :SSD_WIDTH]
    bm = y[:, SSD_WIDTH:SSD_WIDTH + SSD_BC]
    xs_ref[0] = xs.astype(BF16)
    cm_ref[0] = y[:, SSD_WIDTH + SSD_BC:].astype(BF16)
    bt_ref[0] = bm.T.astype(BF16)

    lane = lax.broadcasted_iota(jnp.int32, (TILE, DT_PAD), 1)
    nh2 = 2 * SSD_HEADS
    z = dt_raw + dtb_ref[...]
    dt = jnp.maximum(z, 0.0) + jnp.log1p(jnp.exp(-jnp.abs(z)))
    dt = jnp.where(lane < nh2, dt, 0.0)
    a = dt * (-jnp.exp(alog_ref[...]))
    a = jnp.where(lane < nh2, a, 0.0)

    d_hi, d_mid, d_lo = _split3(dt)
    d3 = (d_hi.astype(F32) + pltpu.roll(d_mid.astype(F32), nh2, axis=1)
          + pltpu.roll(d_lo.astype(F32), 2 * nh2, axis=1)).astype(BF16)
    dexp = _dot(d3, e3_ref[...])
    xdtf_ref[0] = (xs * dexp[:, :SSD_WIDTH]).astype(BF16)
    xdtb_ref[0] = (xs * dexp[:, SSD_WIDTH:]).astype(BF16)

    r = lax.broadcasted_iota(jnp.int32, (SSD_CHUNK, 3 * SSD_CHUNK), 0)
    c = lax.broadcasted_iota(jnp.int32, (SSD_CHUNK, 3 * SSD_CHUNK), 1) % SSD_CHUNK
    tri_l = jnp.where(c <= r, 1.0, 0.0).astype(BF16)
    tri_u = jnp.where(c >= r, 1.0, 0.0).astype(BF16)
    lane_c = lax.broadcasted_iota(jnp.int32, (SSD_CHUNK, DT_PAD), 1)
    cums = []
    for ch in range(TILE // SSD_CHUNK):
        a_c = a[ch * SSD_CHUNK:(ch + 1) * SSD_CHUNK, :]
        a3 = jnp.concatenate(_split3(a_c), axis=0)
        cums.append(jnp.where(lane_c < SSD_HEADS, _dot(tri_l, a3), _dot(tri_u, a3)))
    cum = jnp.concatenate(cums, axis=0) * math.log2(math.e)
    cum_ref[0] = cum
    cumt_ref[0] = cum.T[:cumt_ref.shape[1], :]


def _attn_prep(q, k, v, cs_ref, sn_ref, bd_ref, qg_ref, kg_ref, qt_ref, ko_ref, vt_ref):
    bd = bd_ref[...]
    cs = jnp.concatenate([cs_ref[...]] * (ATTN_WIDTH // LANES), axis=1)
    sn = jnp.concatenate([sn_ref[...]] * (ATTN_WIDTH // LANES), axis=1)
    q = _rope(_head_norm(q, bd, qg_ref[...]), cs, sn)
    q = q * (HEAD_DIM ** -0.5 * math.log2(math.e))
    qt_ref[0] = q.T.astype(BF16)
    k = _rope(_head_norm(k, bd, kg_ref[...]), cs[:, :KV_WIDTH], sn[:, :KV_WIDTH])
    ko_ref[0, 0] = k.astype(BF16)
    vt = v.T.astype(BF16)
    ones_blk = jnp.where(lax.broadcasted_iota(jnp.int32, (V_AUG - HEAD_DIM, TILE), 0) == 0,
                         1.0, 0.0).astype(BF16)
    parts = []
    for g in range(ATTN_KV_HEADS):
        parts += [vt[g * HEAD_DIM:(g + 1) * HEAD_DIM, :], ones_blk]
    vt_ref[0, 0] = jnp.concatenate(parts, axis=0)


def _conformer(cext_ref, sh_ref, gc, cw_ref, cb_ref, lg_ref, lb_ref, pw_ref, pb_ref, o_ref):
    first = HALO - CM_CONV_LEN // 2
    ce = cext_ref[...]
    for ph in range(SUBLANES):
        sh_ref[ph] = pltpu.roll(ce, ce.shape[0] - (first + ph), axis=0)[:sh_ref.shape[1], :]
    acc = jnp.broadcast_to(cb_ref[...], (TILE, CM_WIDTH))
    for k in range(CM_CONV_LEN):
        base = k // SUBLANES * SUBLANES
        acc = acc + sh_ref[k % SUBLANES, base:base + TILE, :] * cw_ref[k:k + 1, :]
    mu = jnp.mean(acc, axis=-1, keepdims=True)
    xc = acc - mu
    var = jnp.mean(xc * xc, axis=-1, keepdims=True)
    u = _silu(xc * lax.rsqrt(var + EPS) * lg_ref[...] + lb_ref[...])
    u = _dot(u.astype(BF16), pw_ref[...]) + pb_ref[...]
    o_ref[0] = (u * _silu(gc)).astype(o_ref.dtype)


def _front_kernel(xc_ref, xl_ref, xp_ref, xn_ref, mod_ref, g_ref, w_ref,
                  scw_ref, scb_ref, dtb_ref, alog_ref, e3_ref,
                  cs_ref, sn_ref, bd_ref, qg_ref, kg_ref,
                  ccw_ref, ccb_ref, lg_ref, lb_ref, pw_ref, pb_ref,
                  z_ref, ga_ref, xs_ref, xdtf_ref, xdtb_ref, cm_ref, bt_ref, cum_ref, cumt_ref,
                  qt_ref, ko_ref, vt_ref, cv_ref,
                  sext_ref, cext_ref, sh_ref, wb_ref):
    d = xl_ref.shape[-1]
    b = pl.program_id(0)
    i = pl.program_id(1)
    nt = pl.num_programs(1)

    @pl.when(jnp.logical_and(b == 0, i == 0))
    def _():
        wb_ref[...] = w_ref[0, :, W_B_START:]

    row = jnp.where(i == 0, CTX_ROW, b)
    m = mod_ref[pl.ds(row, 1), :]
    shift = m[:, :d]
    scale1 = 1.0 + m[:, d:2 * d]
    prev_ok = i >= 2
    next_ok = jnp.logical_and(i != 0, i != nt - 1)

    def mod_norm(x):
        ms = jnp.mean(x * x, axis=-1, keepdims=True)
        return ((x * lax.rsqrt(ms + EPS) * g_ref[...]) * scale1 + shift).astype(BF16)

    x = jnp.where(i == 0, xc_ref[0], xl_ref[0])
    hb = jnp.concatenate([mod_norm(xp_ref[0]), mod_norm(x), mod_norm(xn_ref[0])], axis=0)
    hm = hb[HALO:HALO + TILE]

    def halo_masked(v):
        r = lax.broadcasted_iota(jnp.int32, v.shape, 0)
        ok = jnp.logical_and(jnp.logical_or(r >= HALO, prev_ok), jnp.logical_or(r < HALO + TILE, next_ok))
        return jnp.where(ok, v, 0.0)

    o = PROJ_OFFSETS
    ua = _dot(hb, wb_ref[:, o["ua"]:o["ua"] + CM_WIDTH])
    ub = _dot(hb, wb_ref[:, o["ub"]:o["ub"] + CM_WIDTH])
    cext_ref[...] = halo_masked(ua * jax.nn.sigmoid(ub))
    sext_ref[...] = halo_masked(_dot(hb, w_ref[0, :, o["xbc"]:o["xbc"] + SSD_XBC]))
    dt_raw = _dot(hm, w_ref[0, :, W_DT_START:W_DT_START + DT_PAD])
    q = _dot(hm, wb_ref[:, o["q"]:o["q"] + ATTN_WIDTH])
    k = _dot(hm, wb_ref[:, o["k"]:o["k"] + KV_WIDTH])
    v = _dot(hm, wb_ref[:, o["v"]:o["v"] + KV_WIDTH])
    gc = _dot(hm, wb_ref[:, o["gc"]:o["gc"] + CM_WIDTH])
    z_ref[0] = _dot(hm, w_ref[0, :, o["z"]:o["z"] + SSD_WIDTH]).astype(BF16)
    ga_ref[0] = _dot(hm, wb_ref[:, o["ga"]:o["ga"] + ATTN_WIDTH]).astype(BF16)
    _attn_prep(q, k, v, cs_ref, sn_ref, bd_ref, qg_ref, kg_ref, qt_ref, ko_ref, vt_ref)
    _ssd_prep(sext_ref, dt_raw, scw_ref, scb_ref, dtb_ref, alog_ref, e3_ref,
              xs_ref, xdtf_ref, xdtb_ref, cm_ref, bt_ref, cum_ref, cumt_ref)
    _conformer(cext_ref, sh_ref, gc, ccw_ref, ccb_ref, lg_ref, lb_ref, pw_ref, pb_ref, cv_ref)


def _front(ctx_src, lat_src, lat_off, mod_l, g, w_all, layer, ssd_p, attn_p, cm_p):
    nb, lat_rows, d = lat_src.shape
    nt = 1 + lat_rows // TILE - lat_off
    t = nt * TILE
    hpt = TILE // HALO
    n_hb = lat_rows // HALO

    def lat_tile(i):
        return jnp.maximum(i - 1, 0) + lat_off

    tok = lambda w, dt_: jax.ShapeDtypeStruct((nb, t, w), dt_)
    tile = lambda w: pl.BlockSpec((1, TILE, w), lambda b, i: (b, i, 0))
    const = lambda a: pl.BlockSpec(a.shape, lambda b, i: tuple(0 for _ in a.shape))
    consts = (mod_l, g.reshape(1, d))
    w_spec = pl.BlockSpec((1,) + w_all.shape[1:], lambda b, i: (layer, 0, 0))
    attn_consts = attn_p[2:]
    ctr = 32
    return pl.pallas_call(
        _front_kernel,
        out_shape=(tok(SSD_WIDTH, BF16), tok(ATTN_WIDTH, BF16),
                   tok(SSD_WIDTH, BF16), tok(SSD_WIDTH, BF16), tok(SSD_WIDTH, BF16), tok(SSD_BC, BF16),
                   jax.ShapeDtypeStruct((nb, SSD_BC, t), BF16), tok(DT_PAD, F32),
                   jax.ShapeDtypeStruct((nb, ctr, t), F32),
                   jax.ShapeDtypeStruct((nb, ATTN_WIDTH, t), BF16),
                   jax.ShapeDtypeStruct((nb, nt, TILE, KV_WIDTH), BF16),
                   jax.ShapeDtypeStruct((nb, nt, ATTN_KV_HEADS * V_AUG, TILE), BF16),
                   tok(CM_WIDTH, BF16)),
        grid=(nb, nt),
        in_specs=_row_sources(ctx_src, lat_src, lat_off, d) + [
            pl.BlockSpec((1, HALO, d), lambda b, i: (b, jnp.maximum(lat_tile(i) * hpt - 1, 0), 0)),
            pl.BlockSpec((1, HALO, d), lambda b, i: (b, jnp.minimum((lat_tile(i) + 1) * hpt, n_hb - 1), 0)),
        ] + [const(a) for a in consts] + [w_spec] + [const(a) for a in ssd_p] + [
            pl.BlockSpec((TILE, LANES), lambda b, i: (i, 0)),
            pl.BlockSpec((TILE, LANES), lambda b, i: (i, 0)),
        ] + [const(a) for a in attn_consts] + [const(a) for a in cm_p],
        out_specs=(tile(SSD_WIDTH), tile(ATTN_WIDTH),
                   tile(SSD_WIDTH), tile(SSD_WIDTH), tile(SSD_WIDTH), tile(SSD_BC),
                   pl.BlockSpec((1, SSD_BC, TILE), lambda b, i: (b, 0, i)), tile(DT_PAD),
                   pl.BlockSpec((1, ctr, TILE), lambda b, i: (b, 0, i)),
                   pl.BlockSpec((1, ATTN_WIDTH, TILE), lambda b, i: (b, 0, i)),
                   pl.BlockSpec((1, 1, TILE, KV_WIDTH), lambda b, i: (b, i, 0, 0)),
                   pl.BlockSpec((1, 1, ATTN_KV_HEADS * V_AUG, TILE), lambda b, i: (b, i, 0, 0)),
                   tile(CM_WIDTH)),
        scratch_shapes=[pltpu.VMEM((TILE + 2 * HALO, SSD_XBC), F32),
                        pltpu.VMEM((TILE + 2 * HALO, CM_WIDTH), F32),
                        pltpu.VMEM((SUBLANES, TILE + (CM_CONV_LEN - 1) // SUBLANES * SUBLANES, CM_WIDTH), F32),
                        pltpu.VMEM((d, w_all.shape[2] - W_B_START), BF16)],
        compiler_params=_params(("arbitrary", "arbitrary")),
        name="front",
    )(ctx_src, lat_src, lat_src, lat_src, *consts, w_all, *ssd_p, attn_p[0], attn_p[1], *attn_consts, *cm_p)


def _ssd_direction(xdt, cm, bt, cum, cumt, h_ref, col0, fwd):
    n = SSD_CHUNK
    li = lax.broadcasted_iota(jnp.int32, (n, n), 0)
    si = lax.broadcasted_iota(jnp.int32, (n, n), 1)
    mask = (li >= si) if fwd else (si >= li)
    lo_half = si < SSD_HEAD_DIM
    tot_row = n - 1 if fwd else 0
    hprev = h_ref[...]
    hb = hprev.astype(BF16)
    gw = SSD_WIDTH // SSD_GROUPS
    pairs = SSD_HEADS // SSD_GROUPS // 2
    y_slabs = []
    h_new = []
    for g in range(SSD_GROUPS):
        cg = cm[:, g * SSD_STATE:(g + 1) * SSD_STATE]
        btg = bt[g * SSD_STATE:(g + 1) * SSD_STATE, :]
        cb = _dot(cg, btg)
        yoff = _dot(cg, hb[:, g * gw:(g + 1) * gw])
        xw_slabs = []
        etot_slabs = []
        for jj in range(pairs):
            j = g * pairs + jj
            h0 = col0 + 2 * j
            cb0 = jnp.broadcast_to(cum[:, h0:h0 + 1], (n, n))
            cb1 = jnp.broadcast_to(cum[:, h0 + 1:h0 + 2], (n, n))
            l0 = jnp.exp2(jnp.where(mask, cb0 - cumt[h0:h0 + 1, :], -jnp.inf))
            l1 = jnp.exp2(jnp.where(mask, cb1 - cumt[h0 + 1:h0 + 2, :], -jnp.inf))
            mm = jnp.concatenate([(cb * l0).astype(BF16), (cb * l1).astype(BF16)], axis=1)
            slab = xdt[:, j * LANES:(j + 1) * LANES]
            zero = jnp.zeros_like(slab)
            ww = jnp.concatenate([jnp.where(lo_half, slab, zero), jnp.where(lo_half, zero, slab)], axis=0)
            ydiag = _dot(mm, ww)
            csel = jnp.where(lo_half, cb0, cb1)
            tot = csel[tot_row:tot_row + 1, :]
            y_slabs.append(ydiag + yoff[:, jj * LANES:(jj + 1) * LANES] * jnp.exp2(csel))
            xw_slabs.append((slab.astype(F32) * jnp.exp2(tot - csel)).astype(BF16))
            etot_slabs.append(jnp.exp2(tot))
        xw = jnp.concatenate(xw_slabs, axis=1)
        etot = jnp.concatenate(etot_slabs, axis=1)
        h_new.append(hprev[:, g * gw:(g + 1) * gw] * etot + _dot(btg, xw))
    h_ref[...] = jnp.concatenate(h_new, axis=1)
    return jnp.concatenate(y_slabs, axis=1)


def _ssd_scan_kernel(xf_ref, cf_ref, btf_ref, cumf_ref, cumtf_ref,
                     xb_ref, cb_ref, btb_ref, cumb_ref, cumtb_ref,
                     yf_ref, yb_ref, hf_ref, hb_ref):
    @pl.when(pl.program_id(1) == 0)
    def _():
        hf_ref[...] = jnp.zeros_like(hf_ref)
        hb_ref[...] = jnp.zeros_like(hb_ref)

    n_ch = TILE // SSD_CHUNK
    for ch in range(n_ch):
        r = slice(ch * SSD_CHUNK, (ch + 1) * SSD_CHUNK)
        yf_ref[0, r, :] = _ssd_direction(xf_ref[0, r, :], cf_ref[0, r, :], btf_ref[0, :, r], cumf_ref[0, r, :],
                                         cumtf_ref[0, :, r], hf_ref, 0, True).astype(yf_ref.dtype)
    for ch in reversed(range(n_ch)):
        r = slice(ch * SSD_CHUNK, (ch + 1) * SSD_CHUNK)
        yb_ref[0, r, :] = _ssd_direction(xb_ref[0, r, :], cb_ref[0, r, :], btb_ref[0, :, r], cumb_ref[0, r, :],
                                         cumtb_ref[0, :, r], hb_ref, SSD_HEADS, False).astype(yb_ref.dtype)


def _ssd_scan(xdtf, xdtb, cm, bt, cum, cumt, n_ctx):
    nb, t, _ = xdtf.shape
    nt = t // TILE
    assert n_ctx == TILE
    ctr = cumt.shape[1]

    def bmap(s):
        return jnp.where(s == 0, 0, nt - s)

    fmap = lambda s: s
    rows = lambda w, tm: pl.BlockSpec((1, TILE, w), lambda b, s: (b, tm(s), 0))
    cols = lambda h, tm: pl.BlockSpec((1, h, TILE), lambda b, s: (b, 0, tm(s)))
    side = lambda tm: [rows(SSD_WIDTH, tm), rows(SSD_BC, tm), cols(SSD_BC, tm),
                       rows(DT_PAD, tm), cols(ctr, tm)]
    return pl.pallas_call(
        _ssd_scan_kernel,
        out_shape=(jax.ShapeDtypeStruct((nb, t, SSD_WIDTH), BF16),
                   jax.ShapeDtypeStruct((nb, t, SSD_WIDTH), BF16)),
        grid=(nb, nt),
        in_specs=side(fmap) + side(bmap),
        out_specs=(rows(SSD_WIDTH, fmap), rows(SSD_WIDTH, bmap)),
        scratch_shapes=[pltpu.VMEM((SSD_STATE, SSD_WIDTH), F32), pltpu.VMEM((SSD_STATE, SSD_WIDTH), F32)],
        compiler_params=_params(("arbitrary", "arbitrary")),
        name="ssd_scan",
    )(xdtf, cm, bt, cum, cumt, xdtb, cm, bt, cum, cumt)


def _attn_kernel(qt_ref, k_ref, vt_ref, ga_ref, o_ref, qp_ref, s_ref, p_ref, acc_ref, ot_ref):
    i = pl.program_id(1)
    n_chunks = k_ref.shape[1]
    gq = ATTN_REP * HEAD_DIM
    wq = ATTN_REP * TILE
    groups = range(ATTN_KV_HEADS)
    zero = jnp.zeros((HEAD_DIM, TILE), BF16)
    for g in groups:
        upper = (g * HEAD_DIM) % LANES != 0
        for r in range(ATTN_REP):
            qh = qt_ref[0, g * gq + r * HEAD_DIM:g * gq + (r + 1) * HEAD_DIM, :]
            qp_ref[g, :, r * TILE:(r + 1) * TILE] = jnp.concatenate([zero, qh] if upper else [qh, zero], axis=0)
    acc_ref[...] = jnp.zeros_like(acc_ref)

    def qk_stage(g, c):
        lane_tile = (g * HEAD_DIM) // LANES
        kc = k_ref[0, c, :, lane_tile * LANES:(lane_tile + 1) * LANES]
        s = _dot(kc, qp_ref[g])
        s_ref[g] = s
        return jnp.max(s, axis=0, keepdims=True)

    def pv_stage(g, c, a_prev):
        vc = vt_ref[0, c, g * V_AUG:(g + 1) * V_AUG, :]
        acc_ref[g] = acc_ref[g] * a_prev + _dot(vc, p_ref[g])

    def step(c, carry, do_pv, do_qk):
        ms, a_prevs, cmaxs = carry
        new_m, new_a, new_cmax = [], [], []
        for g in groups:
            if do_pv:
                pv_stage(g, c - 1, a_prevs[g])
            m_new = jnp.maximum(ms[g], cmaxs[g])
            p_ref[g] = jnp.exp2(s_ref[g] - m_new).astype(BF16)
            new_cmax.append(qk_stage(g, c + 1) if do_qk else cmaxs[g])
            new_m.append(m_new)
            new_a.append(jnp.exp2(ms[g] - m_new))
        return tuple(new_m), tuple(new_a), tuple(new_cmax)

    def run(nk):
        carry = (tuple(jnp.full((1, wq), -jnp.inf, F32) for _ in groups),
                 tuple(jnp.ones((1, wq), F32) for _ in groups),
                 tuple(qk_stage(g, 0) for g in groups))
        carry = step(0, carry, False, nk > 1)
        if nk > 2:
            carry = lax.fori_loop(1, nk - 1, lambda c, cr: step(c, cr, True, True), carry,
                                  unroll=ATTN_UNROLL)
        if nk > 1:
            carry = step(nk - 1, carry, True, False)
        for g in groups:
            pv_stage(g, nk - 1, carry[1][g])
            acc = acc_ref[g]
            ot = acc[:HEAD_DIM, :] / acc[HEAD_DIM:HEAD_DIM + 1, :]
            for r in range(ATTN_REP):
                ot_ref[g * gq + r * HEAD_DIM:g * gq + (r + 1) * HEAD_DIM, :] = ot[:, r * TILE:(r + 1) * TILE]
        o_ref[0] = (ot_ref[...].T * _silu(ga_ref[0].astype(F32))).astype(o_ref.dtype)

    @pl.when(i == 0)
    def _():
        run(1)

    @pl.when(i != 0)
    def _():
        run(n_chunks)


def _attention(qt, kk, vt, ga):
    nb, t, _ = ga.shape
    nt = t // TILE
    return pl.pallas_call(
        _attn_kernel,
        out_shape=jax.ShapeDtypeStruct((nb, t, ATTN_WIDTH), BF16),
        grid=(nb, nt),
        in_specs=[pl.BlockSpec((1, ATTN_WIDTH, TILE), lambda b, i: (b, 0, i)),
                  pl.BlockSpec((1, nt, TILE, KV_WIDTH), lambda b, i: (b, 0, 0, 0)),
                  pl.BlockSpec((1, nt, ATTN_KV_HEADS * V_AUG, TILE), lambda b, i: (b, 0, 0, 0)),
                  pl.BlockSpec((1, TILE, ATTN_WIDTH), lambda b, i: (b, i, 0))],
        out_specs=pl.BlockSpec((1, TILE, ATTN_WIDTH), lambda b, i: (b, i, 0)),
        scratch_shapes=[pltpu.VMEM((ATTN_KV_HEADS, LANES, ATTN_REP * TILE), BF16),
                        pltpu.VMEM((ATTN_KV_HEADS, TILE, ATTN_REP * TILE), F32),
                        pltpu.VMEM((ATTN_KV_HEADS, TILE, ATTN_REP * TILE), BF16),
                        pltpu.VMEM((ATTN_KV_HEADS, V_AUG, ATTN_REP * TILE), F32),
                        pltpu.VMEM((ATTN_WIDTH, TILE), F32)],
        compiler_params=_params(("arbitrary", "arbitrary")),
        name="attention",
    )(qt, kk, vt, ga)


def _outproj_kernel(xc_ref, xl_ref, yf_ref, yb_ref, xs_ref, z_ref, at_ref, cv_ref, mod_ref,
                    dsk_ref, ng_ref, w_ref, fg_ref, o_ref, *, tile_off, final):
    d = xl_ref.shape[-1]
    b = pl.program_id(0)
    i = pl.program_id(1) + tile_off
    row = jnp.where(i == 0, CTX_ROW, b)
    x = xl_ref[0] if final else jnp.where(i == 0, xc_ref[0], xl_ref[0])
    gate = mod_ref[pl.ds(row, 1), :][:, 2 * d:]
    y = yf_ref[0].astype(F32) + yb_ref[0].astype(F32) + dsk_ref[...] * xs_ref[0].astype(F32)
    y = y * _silu(z_ref[0].astype(F32))
    y = y * lax.rsqrt(jnp.mean(y * y, axis=-1, keepdims=True) + EPS) * ng_ref[...]
    cat = jnp.concatenate([y.astype(BF16), at_ref[0], cv_ref[0]], axis=1)
    xn = x + gate * _dot(cat, w_ref[...])
    if final:
        xn = xn * lax.rsqrt(jnp.mean(xn * xn, axis=-1, keepdims=True) + EPS) * fg_ref[...]
    o_ref[0] = xn


def _outproj(ctx_src, lat_src, lat_off, yf, yb, xs, z, at, cv, mod_l, d_skip_exp, norm_g, w_out,
             final_g, final):
    nb, t, _ = yf.shape
    d = lat_src.shape[-1]
    tile_off = 1 if final else 0
    nt = t // TILE - tile_off
    if final:
        x_specs = [pl.BlockSpec((1, TILE, d), lambda b, i: (b, 0, 0)),
                   pl.BlockSpec((1, TILE, d), lambda b, i: (b, i + lat_off, 0))]
    else:
        x_specs = _row_sources(ctx_src, lat_src, lat_off, d)
    tile = lambda w: pl.BlockSpec((1, TILE, w), lambda b, i: (b, i + tile_off, 0))
    const = lambda shp: pl.BlockSpec(shp, lambda b, i: tuple(0 for _ in shp))
    return pl.pallas_call(
        functools.partial(_outproj_kernel, tile_off=tile_off, final=final),
        out_shape=jax.ShapeDtypeStruct((nb, nt * TILE, d), F32),
        grid=(nb, nt),
        in_specs=x_specs + [tile(SSD_WIDTH), tile(SSD_WIDTH), tile(SSD_WIDTH), tile(SSD_WIDTH),
                  tile(ATTN_WIDTH), tile(CM_WIDTH), const(mod_l.shape),
                  const((1, SSD_WIDTH)), const((1, SSD_WIDTH)), const(w_out.shape), const((1, d))],
        out_specs=pl.BlockSpec((1, TILE, d), lambda b, i: (b, i, 0)),
        compiler_params=_params(("arbitrary", "arbitrary")),
        name="out_proj_final" if final else "out_proj",
    )(ctx_src, lat_src, yf, yb, xs, z, at, cv, mod_l, d_skip_exp.reshape(1, SSD_WIDTH),
      norm_g.reshape(1, SSD_WIDTH), w_out, final_g.reshape(1, d))


def _expand_matrix():
    nh2 = 2 * SSD_HEADS
    e = np.zeros((DT_PAD, 2 * SSD_WIDTH), np.float32)
    for src in range(nh2):
        for term in range(3):
            e[term * nh2 + src, src * SSD_HEAD_DIM:(src + 1) * SSD_HEAD_DIM] = 1.0
    return jnp.asarray(e, BF16)


def _block_ones(width):
    idx = np.arange(width) // HEAD_DIM
    return jnp.asarray((idx[:, None] == idx[None, :]).astype(np.float32), BF16)


def _rope_tables(seq, n_ctx):
    n_rows = seq // GRID_W
    rows = jnp.repeat(jnp.arange(n_rows, dtype=F32), GRID_W)
    cols = jnp.tile(jnp.arange(GRID_W, dtype=F32), n_rows)
    inv = ROPE_BASE ** (-jnp.arange(ROPE_NF, dtype=F32) / ROPE_NF)
    ar, ac = rows[:, None] * inv, cols[:, None] * inv
    cs = jnp.concatenate([jnp.cos(ar), jnp.cos(ar), jnp.cos(ac), jnp.cos(ac)], axis=1)
    sn = jnp.concatenate([-jnp.sin(ar), jnp.sin(ar), -jnp.sin(ac), jnp.sin(ac)], axis=1)
    cs = jnp.concatenate([jnp.ones((n_ctx, HEAD_DIM), F32), cs], axis=0)
    sn = jnp.concatenate([jnp.zeros((n_ctx, HEAD_DIM), F32), sn], axis=0)
    return jnp.tile(cs, (1, LANES // HEAD_DIM)), jnp.tile(sn, (1, LANES // HEAD_DIM))


def kernel(x, c, ctx, c_ctx, norm_g, ada_w, ada_b, w_in, ssd_conv_w, ssd_conv_b, ssd_dt_bias, ssd_a_log, ssd_d, ssd_norm_g, q_norm_g, k_norm_g, cm_conv_w, cm_conv_b, cm_ln_g, cm_ln_b, cm_pw_w, cm_pw_b, w_out, final_norm_g):
    nb, seq, d = x.shape
    n_ctx = ctx.shape[1]
    depth = w_in.shape[0]
    assert n_ctx == TILE and seq % TILE == 0 and seq % GRID_W == 0 and nb <= CTX_ROW
    nh2 = 2 * SSD_HEADS

    cvec = jnp.zeros((MOD_ROWS, d), F32).at[:nb].set(c).at[CTX_ROW].set(c_ctx)
    mod = _ada(cvec, ada_w, ada_b)
    cs, sn = _rope_tables(seq, n_ctx)
    e3 = _expand_matrix()
    bd = _block_ones(2 * LANES)
    w_bf = w_in.astype(BF16)

    src = (ctx, x, 0)
    for l in range(depth):
        final = l == depth - 1
        pad_rows = lambda a, n: jnp.zeros((n, a.shape[1]), F32).at[:a.shape[0]].set(a)
        pad_dt = lambda a: jnp.zeros((1, DT_PAD), F32).at[0, :nh2].set(a)
        ssd_p = (pad_rows(ssd_conv_w[l], SUBLANES), ssd_conv_b[l].reshape(1, SSD_XBC),
                 pad_dt(ssd_dt_bias[l]), pad_dt(ssd_a_log[l]), e3)
        attn_p = (cs, sn, bd, jnp.tile(q_norm_g[l], ATTN_HEADS).reshape(1, ATTN_WIDTH),
                  jnp.tile(k_norm_g[l], ATTN_KV_HEADS).reshape(1, KV_WIDTH))
        cm_row = lambda a: a.reshape(1, CM_WIDTH)
        cm_p = (pad_rows(cm_conv_w[l], 4 * SUBLANES), cm_row(cm_conv_b[l]), cm_row(cm_ln_g[l]),
                cm_row(cm_ln_b[l]), cm_pw_w[l].astype(BF16), cm_row(cm_pw_b[l]))
        z, ga, xs, xdtf, xdtb, cm, bt, cum, cumt, qt, kk, vt, cv = _front(
            *src, mod[l], norm_g[l], w_bf, l, ssd_p, attn_p, cm_p)
        yf, yb = _ssd_scan(xdtf, xdtb, cm, bt, cum, cumt, n_ctx)
        at = _attention(qt, kk, vt, ga)
        out = _outproj(*src, yf, yb, xs, z, at, cv, mod[l], jnp.repeat(ssd_d[l], SSD_HEAD_DIM),
                       ssd_norm_g[l], w_out[l].astype(BF16), final_norm_g, final)
        src = (out, out, 1)
    return out
```
